```python
import math
import jax, jax.numpy as jnp
from jax import lax
import numpy as np

D_MODEL = 1024
BATCH = 8
SEQ = 2048
DEPTH = 4
DEC_BATCH = 128
DEC_SEQ = 8
PAST_LEN = 2048
PAGE_SIZE = 128

LRU_WIDTH = 512
LRU_BLOCKS = 8
LRU_BLOCK_DIM = LRU_WIDTH // LRU_BLOCKS
CONV_WIDTH = 4
LRU_C = 8.0
N_HEADS = 8
HEAD_DIM = 64
ATTN_WIDTH = N_HEADS * HEAD_DIM
MIX_WIDTH = LRU_WIDTH + ATTN_WIDTH
ATTN_SCALE = HEAD_DIM ** -0.5
DILATED_PAIRS = ((128, 1), (512, 4), (2048, 16))
WINDOW_MAX = 2048
N_EXPERTS = 32
TOP_K = 4
D_FF = 1024
SWIGLU_ALPHA = 1.702
SWIGLU_LIMIT = 7.0
MOE_BLOCK = 128
PLE_DIM = 256
DEEPNORM_ALPHA = (2.0 * DEPTH) ** 0.25
DEEPNORM_BETA = (8.0 * DEPTH) ** -0.25
LN_EPS = 1e-5
RMS_EPS = 1e-6

kernel_name = "hymba_rglru_dilated_alibi_moe_step"


def _layer_norm(x, g, b):
    xf = x.astype(jnp.float32)
    mu = jnp.mean(xf, axis=-1, keepdims=True)
    var = jnp.mean(jnp.square(xf - mu), axis=-1, keepdims=True)
    return ((xf - mu) * lax.rsqrt(var + LN_EPS) * g + b).astype(x.dtype)


def _rms_norm(x, g):
    xf = x.astype(jnp.float32)
    return (xf * lax.rsqrt(jnp.mean(jnp.square(xf), axis=-1, keepdims=True) + RMS_EPS) * g).astype(x.dtype)


def _block_diag(x, w):
    B, T, _ = x.shape
    xb = x.reshape(B, T, LRU_BLOCKS, LRU_BLOCK_DIM)
    return jnp.einsum('btnc,ncd->btnd', xb, w).reshape(B, T, LRU_WIDTH)


def _rg_lru_branch(u, gate, conv_state, h0, conv_w, conv_b, w_ga, b_ga, w_gx, b_gx, lam):
    T = u.shape[1]
    up = jnp.concatenate([conv_state.astype(u.dtype), u], axis=1)
    xc = conv_b
    for tap in range(CONV_WIDTH):
        xc = xc + conv_w[tap] * up[:, tap:tap + T]
    new_conv = up[:, T:]
    r = jax.nn.sigmoid((_block_diag(xc, w_ga) + b_ga).astype(jnp.float32))
    i = jax.nn.sigmoid((_block_diag(xc, w_gx) + b_gx).astype(jnp.float32))
    log_a = -LRU_C * r * jax.nn.softplus(-lam.astype(jnp.float32))
    a = jnp.exp(log_a)
    b_in = jnp.sqrt(-jnp.expm1(2.0 * log_a)) * i * xc.astype(jnp.float32)

    def step(h, ab):
        a_t, b_t = ab
        h = a_t * h + b_t
        return h, h

    h_last, hs = lax.scan(step, h0.astype(jnp.float32), (jnp.swapaxes(a, 0, 1), jnp.swapaxes(b_in, 0, 1)))
    y = jnp.swapaxes(hs, 0, 1) * jax.nn.gelu(gate.astype(jnp.float32))
    return y.astype(u.dtype), new_conv, h_last.astype(h0.dtype)


def _alibi_slopes():
    return jnp.asarray([2.0 ** (-8.0 * (h + 1) / N_HEADS) for h in range(N_HEADS)], dtype=jnp.float32)


def _merge_by_denominator(outs, lses):
    w = jax.nn.softmax(jnp.stack(lses, axis=0), axis=0)
    return jnp.sum(w[..., None] * jnp.stack(outs, axis=0), axis=0)


def _dilated_band_prompt(q, k, v, window, dil, slopes):
    B, S, H, Dh = q.shape
    n = S // dil
    nw = window // dil
    blk = nw
    nb = -(-n // blk)
    pad = nb * blk - n

    def split(t):
        t = t.reshape(B, n, dil, H, Dh)
        t = jnp.pad(t, ((0, 0), (0, pad), (0, 0), (0, 0), (0, 0)))
        return t.reshape(B, nb, blk, dil, H, Dh)

    def with_prev(t):
        prev = jnp.pad(t, ((0, 0), (1, 0), (0, 0), (0, 0), (0, 0), (0, 0)))[:, :-1]
        return jnp.concatenate([prev, t], axis=2)

    qb = split(q * ATTN_SCALE)
    kk = with_prev(split(k))
    vv = with_prev(split(v))
    s = jnp.einsum('bnqrhc,bnkrhc->bnrhqk', qb, kk, preferred_element_type=jnp.float32)
    qi = jnp.arange(blk)[:, None] + blk
    ki = jnp.arange(2 * blk)[None, :]
    steps = qi - ki
    key_sub = (jnp.arange(nb) * blk)[:, None, None] + ki[None] - blk
    valid = (steps >= 0)[None] & (steps <= nw)[None] & (key_sub >= 0)
    bias = -slopes[:, None, None] * (dil * steps).astype(jnp.float32)[None]
    s = jnp.where(valid[None, :, None, None], s + bias, -jnp.inf)
    m = jnp.max(s, axis=-1, keepdims=True)
    e = jnp.exp(s - m)
    l = jnp.sum(e, axis=-1, keepdims=True)
    o = jnp.einsum('bnrhqk,bnkrhc->bnqrhc', e / l, vv, preferred_element_type=jnp.float32)
    lse = (m + jnp.log(l))[..., 0]
    o = o.reshape(B, nb * blk, dil, H, Dh)[:, :n].reshape(B, S, H, Dh)
    lse = jnp.transpose(lse, (0, 1, 4, 2, 3)).reshape(B, nb * blk, dil, H)[:, :n].reshape(B, S, H)
    return o, lse


def _dilated_attention_prompt(q, k, v, slopes):
    outs, lses = [], []
    for window, dil in DILATED_PAIRS:
        o, lse = _dilated_band_prompt(q, k, v, window, dil, slopes)
        outs.append(o)
        lses.append(lse)
    return _merge_by_denominator(outs, lses).astype(q.dtype)


def _dilated_attention_sample(q, k, v, k_buf, v_buf, slopes):
    T = q.shape[1]
    W = k_buf.shape[1]
    k_all = jnp.concatenate([k_buf.astype(k.dtype), k], axis=1)
    v_all = jnp.concatenate([v_buf.astype(v.dtype), v], axis=1)
    qs = q * ATTN_SCALE
    outs, lses = [], []
    for window, dil in DILATED_PAIRS:
        j = jnp.arange(window // dil + 1)
        idx = W + jnp.arange(T)[:, None] - dil * j[None, :]
        valid = idx >= 0
        idx_c = jnp.maximum(idx, 0)
        kg = k_all[:, idx_c]
        vg = v_all[:, idx_c]
        s = jnp.einsum('bthc,btjhc->bhtj', qs, kg, preferred_element_type=jnp.float32)
        s = s - slopes[:, None, None] * (dil * j).astype(jnp.float32)[None, None, :]
        s = jnp.where(valid[None, None], s, -jnp.inf)
        m = jnp.max(s, axis=-1, keepdims=True)
        e = jnp.exp(s - m)
        l = jnp.sum(e, axis=-1, keepdims=True)
        outs.append(jnp.einsum('bhtj,btjhc->bthc', e / l, vg, preferred_element_type=jnp.float32))
        lses.append(jnp.transpose((m + jnp.log(l))[..., 0], (0, 2, 1)))
    return _merge_by_denominator(outs, lses).astype(q.dtype)


def _swiglu_clamped(h):
    glu, lin = jnp.split(h, 2, axis=-1)
    glu = jnp.minimum(glu, SWIGLU_LIMIT)
    lin = jnp.clip(lin, -SWIGLU_LIMIT, SWIGLU_LIMIT)
    return glu * jax.nn.sigmoid(SWIGLU_ALPHA * glu) * (lin + 1.0)


def _moe(x, w_router, b_router, w_up, b_up, w_down, b_down):
    B, T, D = x.shape
    n_tok = B * T
    xf = x.reshape(n_tok, D)
    logits = (xf @ w_router + b_router).astype(jnp.float32)
    top_vals, top_idx = lax.top_k(logits, TOP_K)
    gates = jax.nn.softmax(top_vals, axis=-1)
    n_assign = n_tok * TOP_K
    expert_of = top_idx.reshape(-1).astype(jnp.int32)
    token_of = jnp.arange(n_assign, dtype=jnp.int32) // TOP_K
    order = jnp.argsort(expert_of)
    e_sorted = expert_of[order]
    tok_sorted = token_of[order]
    gate_sorted = gates.reshape(-1)[order]
    counts = jnp.bincount(expert_of, length=N_EXPERTS)
    starts = jnp.cumsum(counts) - counts
    padded = (counts + MOE_BLOCK - 1) // MOE_BLOCK * MOE_BLOCK
    padded_ends = jnp.cumsum(padded)
    padded_starts = padded_ends - padded
    dest = padded_starts[e_sorted] + (jnp.arange(n_assign, dtype=jnp.int32) - starts[e_sorted])
    n_blocks = -(-n_assign // MOE_BLOCK) + N_EXPERTS
    n_rows = n_blocks * MOE_BLOCK
    row_token = jnp.full((n_rows,), n_tok, dtype=jnp.int32).at[dest].set(tok_sorted)
    x_pad = jnp.concatenate([xf, jnp.zeros((1, D), xf.dtype)], axis=0)
    xs = x_pad[row_token].reshape(n_blocks, MOE_BLOCK, D)
    block_expert = jnp.minimum(
        jnp.searchsorted(padded_ends, jnp.arange(n_blocks, dtype=jnp.int32) * MOE_BLOCK, side='right'),
        N_EXPERTS - 1)

    def expert_block(args):
        xb, e = args
        h = xb @ w_up[e] + b_up[e]
        return _swiglu_clamped(h) @ w_down[e] + b_down[e]

    ys = lax.map(expert_block, (xs, block_expert)).reshape(n_rows, D)
    contrib = ys[dest] * gate_sorted[:, None].astype(ys.dtype)
    y = jax.ops.segment_sum(contrib, tok_sorted, num_segments=n_tok)
    return y.reshape(B, T, D).astype(x.dtype)


def _layer(x, ple, conv_state, h0, attend, lp):
    B, T, _ = x.shape
    proj = x @ lp['w_in']
    u, gate, q, k, v = jnp.split(
        proj, [LRU_WIDTH, 2 * LRU_WIDTH, 2 * LRU_WIDTH + ATTN_WIDTH, 2 * LRU_WIDTH + 2 * ATTN_WIDTH], axis=-1)
    y_lru, new_conv, new_h = _rg_lru_branch(u, gate, conv_state, h0, lp['conv_w'], lp['conv_b'],
                                            lp['w_gate_a'], lp['b_gate_a'], lp['w_gate_x'], lp['b_gate_x'],
                                            lp['lru_lambda'])
    q = q.reshape(B, T, N_HEADS, HEAD_DIM)
    k = k.reshape(B, T, N_HEADS, HEAD_DIM)
    v = v.reshape(B, T, N_HEADS, HEAD_DIM)
    o_attn = attend(q, k, v).reshape(B, T, ATTN_WIDTH)
    mixed = jnp.concatenate([_rms_norm(y_lru, lp['g_lru_norm']), _rms_norm(o_attn, lp['g_attn_norm'])],
                            axis=-1) @ lp['w_out']
    x = _layer_norm(DEEPNORM_ALPHA * x + mixed, lp['ln1_g'], lp['ln1_b'])
    ffn = _moe(x, lp['w_router'], lp['b_router'], lp['w_up'], lp['b_up'], lp['w_down'], lp['b_down'])
    x = _layer_norm(DEEPNORM_ALPHA * x + ffn, lp['ln2_g'], lp['ln2_b'])
    x = x + jax.nn.sigmoid((x @ lp['w_ple_gate']).astype(jnp.float32)).astype(x.dtype) * (ple @ lp['w_ple_proj'])
    return x, k, v, new_conv, new_h


def setup_inputs(seed: int = 0) -> dict:
    key = jax.random.key(seed)
    ks = jax.random.split(key, 32)
    w_buf = min(WINDOW_MAX, PAST_LEN)

    def nrm(k, shape, scale):
        return jax.random.normal(k, shape, jnp.float32) * scale

    a0 = jax.random.uniform(ks[0], (DEPTH, LRU_WIDTH), jnp.float32, 0.9, 0.999) ** (1.0 / LRU_C)
    lru_lambda = jnp.log(a0) - jnp.log1p(-a0)
    return {
        'x_prompt': nrm(ks[1], (BATCH, SEQ, D_MODEL), 1.0),
        'x_sample': nrm(ks[2], (DEC_BATCH, DEC_SEQ, D_MODEL), 1.0),
        'cache_k': nrm(ks[3], (DEPTH, DEC_BATCH, w_buf, N_HEADS, HEAD_DIM), 1.0),
        'cache_v': nrm(ks[4], (DEPTH, DEC_BATCH, w_buf, N_HEADS, HEAD_DIM), 1.0),
        'state_conv': nrm(ks[5], (DEPTH, DEC_BATCH, CONV_WIDTH - 1, LRU_WIDTH), 1.0),
        'state_h': nrm(ks[6], (DEPTH, DEC_BATCH, LRU_WIDTH), 0.5),
        'p_prompt': nrm(ks[7], (DEPTH, BATCH, SEQ, PLE_DIM), 1.0),
        'p_sample': nrm(ks[8], (DEPTH, DEC_BATCH, DEC_SEQ, PLE_DIM), 1.0),
        'w_in': nrm(ks[9], (DEPTH, D_MODEL, 2 * LRU_WIDTH + 3 * ATTN_WIDTH), D_MODEL ** -0.5),
        'conv_w': nrm(ks[10], (DEPTH, CONV_WIDTH, LRU_WIDTH), CONV_WIDTH ** -0.5),
        'conv_b': nrm(ks[11], (DEPTH, LRU_WIDTH), 0.01),
        'w_gate_a': nrm(ks[12], (DEPTH, LRU_BLOCKS, LRU_BLOCK_DIM, LRU_BLOCK_DIM), LRU_BLOCK_DIM ** -0.5),
        'b_gate_a': nrm(ks[13], (DEPTH, LRU_WIDTH), 0.01),
        'w_gate_x': nrm(ks[14], (DEPTH, LRU_BLOCKS, LRU_BLOCK_DIM, LRU_BLOCK_DIM), LRU_BLOCK_DIM ** -0.5),
        'b_gate_x': nrm(ks[15], (DEPTH, LRU_WIDTH), 0.01),
        'lru_lambda': lru_lambda,
        'g_lru_norm': 1.0 + nrm(ks[16], (DEPTH, LRU_WIDTH), 0.01),
        'g_attn_norm': 1.0 + nrm(ks[17], (DEPTH, ATTN_WIDTH), 0.01),
        'w_out': nrm(ks[18], (DEPTH, MIX_WIDTH, D_MODEL), DEEPNORM_BETA * MIX_WIDTH ** -0.5),
        'ln1_g': 1.0 + nrm(ks[19], (DEPTH, D_MODEL), 0.01),
        'ln1_b': nrm(ks[20], (DEPTH, D_MODEL), 0.01),
        'w_router': nrm(ks[21], (DEPTH, D_MODEL, N_EXPERTS), D_MODEL ** -0.5),
        'b_router': nrm(ks[22], (DEPTH, N_EXPERTS), 0.01),
        'w_up': nrm(ks[23], (DEPTH, N_EXPERTS, D_MODEL, 2 * D_FF), D_MODEL ** -0.5),
        'b_up': nrm(ks[24], (DEPTH, N_EXPERTS, 2 * D_FF), 0.01),
        'w_down': nrm(ks[25], (DEPTH, N_EXPERTS, D_FF, D_MODEL), DEEPNORM_BETA * D_FF ** -0.5),
        'b_down': nrm(ks[26], (DEPTH, N_EXPERTS, D_MODEL), 0.01),
        'ln2_g': 1.0 + nrm(ks[27], (DEPTH, D_MODEL), 0.01),
        'ln2_b': nrm(ks[28], (DEPTH, D_MODEL), 0.01),
        'w_ple_proj': nrm(ks[29], (DEPTH, PLE_DIM, D_MODEL), PLE_DIM ** -0.5),
        'w_ple_gate': nrm(ks[30], (DEPTH, D_MODEL, D_MODEL), D_MODEL ** -0.5),
    }


def reference(x_prompt, x_sample, cache_k, cache_v, state_conv, state_h, p_prompt, p_sample,
              w_in, conv_w, conv_b, w_gate_a, b_gate_a, w_gate_x, b_gate_x, lru_lambda,
              g_lru_norm, g_attn_norm, w_out, ln1_g, ln1_b, w_router, b_router, w_up, b_up,
              w_down, b_down, ln2_g, ln2_b, w_ple_proj, w_ple_gate):
    slopes = _alibi_slopes()
    bp = x_prompt.shape[0]
    w_prompt = min(WINDOW_MAX, x_prompt.shape[1])
    conv0 = jnp.zeros((bp, CONV_WIDTH - 1, LRU_WIDTH), x_prompt.dtype)
    h_zero = jnp.zeros((bp, LRU_WIDTH), x_prompt.dtype)

    def attend_prompt(q, k, v):
        return _dilated_attention_prompt(q, k, v, slopes)

    xp, xs = x_prompt, x_sample
    kp_l, vp_l, cp_l, hp_l, ks_l, vs_l, cs_l, hs_l = [], [], [], [], [], [], [], []
    for l in range(DEPTH):
        lp = {
            'w_in': w_in[l], 'conv_w': conv_w[l], 'conv_b': conv_b[l],
            'w_gate_a': w_gate_a[l], 'b_gate_a': b_gate_a[l], 'w_gate_x': w_gate_x[l], 'b_gate_x': b_gate_x[l],
            'lru_lambda': lru_lambda[l], 'g_lru_norm': g_lru_norm[l], 'g_attn_norm': g_attn_norm[l],
            'w_out': w_out[l], 'ln1_g': ln1_g[l], 'ln1_b': ln1_b[l], 'w_router': w_router[l],
            'b_router': b_router[l], 'w_up': w_up[l], 'b_up': b_up[l], 'w_down': w_down[l],
            'b_down': b_down[l], 'ln2_g': ln2_g[l], 'ln2_b': ln2_b[l],
            'w_ple_proj': w_ple_proj[l], 'w_ple_gate': w_ple_gate[l],
        }
        xp, kp, vp, cp, hp = _layer(xp, p_prompt[l], conv0, h_zero, attend_prompt, lp)
        kb, vb = cache_k[l], cache_v[l]

        def attend_sample(q, k, v, kb=kb, vb=vb):
            return _dilated_attention_sample(q, k, v, kb, vb, slopes)

        xs, ksn, vsn, csn, hsn = _layer(xs, p_sample[l], state_conv[l], state_h[l], attend_sample, lp)
        kp_l.append(kp[:, -w_prompt:])
        vp_l.append(vp[:, -w_prompt:])
        cp_l.append(cp)
        hp_l.append(hp)
        ks_l.append(ksn)
        vs_l.append(vsn)
        cs_l.append(csn)
        hs_l.append(hsn)
    return (xp, xs, jnp.stack(kp_l), jnp.stack(vp_l), jnp.stack(cp_l), jnp.stack(hp_l),
            jnp.stack(ks_l), jnp.stack(vs_l), jnp.stack(cs_l), jnp.stack(hs_l))
```

```python
import functools

import numpy as np
import jax
import jax.numpy as jnp
from jax import lax
from jax.experimental import pallas as pl
from jax.experimental.pallas import tpu as pltpu

F32 = jnp.float32
BF16 = jnp.bfloat16
I32 = jnp.int32

D_MODEL = 1024
LRU_WIDTH = 512
LRU_BLOCK_DIM = 64
CONV_WIDTH = 4
LRU_C = 8.0
N_HEADS = 8
HEAD_DIM = 64
ATTN_WIDTH = N_HEADS * HEAD_DIM
ATTN_SCALE = HEAD_DIM ** -0.5
ATTN_BLK = 128
DILATIONS = (1, 4, 16)
N_EXPERTS = 32
TOP_K = 4
D_FF = 1024
SWIGLU_ALPHA = 1.702
SWIGLU_LIMIT = 7.0
PLE_DIM = 256
MODEL_DEPTH = 4
DEEPNORM_ALPHA = (2.0 * MODEL_DEPTH) ** 0.25
LN_EPS = 1e-5
RMS_EPS = 1e-6

SUBLANES = 8
LANES = 128
VMEM_LIMIT_BYTES = 56 * 1024 * 1024

TOKEN_TILE = 512
LRU_CHUNK = 256
EXPERT_TILE = 256
COMBINE_TILE = 256
DISPATCH_TILE = 1024


def _params(*sem):
    return pltpu.CompilerParams(dimension_semantics=sem, vmem_limit_bytes=VMEM_LIMIT_BYTES)


def _alibi_slopes():
    return [2.0 ** (-8.0 * (h + 1) / N_HEADS) for h in range(N_HEADS)]


def _inproj_kernel(x_ref, w_ref, u_ref, g_ref, q_ref, k_ref, v_ref):
    xb = x_ref[...].astype(BF16)
    for i, o_ref in enumerate((u_ref, g_ref, q_ref, k_ref, v_ref)):
        o_ref[...] = jnp.dot(xb, w_ref[0, :, i * 512:(i + 1) * 512], preferred_element_type=F32)


def _inproj(x, w_in_bf, layer):
    n = x.shape[0]
    out = jax.ShapeDtypeStruct((n, 512), F32)
    return pl.pallas_call(
        _inproj_kernel,
        grid=(n // TOKEN_TILE,),
        in_specs=[pl.BlockSpec((TOKEN_TILE, D_MODEL), lambda i: (i, 0)),
                  pl.BlockSpec((1, D_MODEL, 2560), lambda i: (layer, 0, 0))],
        out_specs=[pl.BlockSpec((TOKEN_TILE, 512), lambda i: (i, 0))] * 5,
        out_shape=[out] * 5,
        compiler_params=_params("parallel"),
        name="inproj",
    )(x, w_in_bf)


def _lru_coeffs(xc, p, w2_ref):
    xb = xc.astype(BF16)
    g0 = jnp.dot(xb[:, :256], w2_ref[0, 0], preferred_element_type=F32)
    g1 = jnp.dot(xb[:, 256:], w2_ref[0, 1], preferred_element_type=F32)
    ga = jnp.concatenate([g0[:, :256], g1[:, :256]], axis=1) + p[5:6, :]
    gx = jnp.concatenate([g0[:, 256:], g1[:, 256:]], axis=1) + p[6:7, :]
    r = jax.nn.sigmoid(ga)
    i = jax.nn.sigmoid(gx)
    z = -p[7:8, :]
    softplus = jnp.maximum(z, 0.0) + jnp.log1p(jnp.exp(-jnp.abs(z)))
    log_a = (-LRU_C) * r * softplus
    a = jnp.exp(log_a)
    b = jnp.sqrt(-jnp.tanh(log_a) * (a * a + 1.0)) * i * xc
    return a, b


def _group_scan(a, b):
    row = lax.broadcasted_iota(I32, a.shape, 0) & (SUBLANES - 1)
    for s in (1, 2, 4):
        a_prev = pltpu.roll(a, s, 0)
        b_prev = pltpu.roll(b, s, 0)
        m = row >= s
        b = jnp.where(m, a * b_prev + b, b)
        a = jnp.where(m, a * a_prev, a)
    return a, b


def _lru_finish(h, gate, p):
    y = h * jax.nn.gelu(gate)
    var = jnp.mean(y * y, axis=-1, keepdims=True)
    return (y * lax.rsqrt(var + RMS_EPS) * p[8:9, :]).astype(BF16)


def _lru_prompt_kernel(u_ref, gate_ref, cs_ref, h0_ref, p_ref, w2_ref, y_ref, hlast_ref, tail_s, hc_s):
    c = pl.program_id(1)

    @pl.when(c == 0)
    def _():
        tail_s[...] = cs_ref[0]
        hc_s[...] = h0_ref[0]

    p = p_ref[0]
    u = u_ref[...]
    tail = tail_s[...]
    row8 = lax.broadcasted_iota(I32, (SUBLANES, LRU_WIDTH), 0)
    xc = p[4:5, :] + p[3:4, :] * u
    for s in (1, 2, 3):
        sh = pltpu.roll(u, s, 0)
        first = jnp.where(row8 < s, pltpu.roll(tail, s, 0), sh[0:SUBLANES])
        xc = xc + p[3 - s:4 - s, :] * jnp.concatenate([first, sh[SUBLANES:]], axis=0)
    tail_s[...] = u[LRU_CHUNK - SUBLANES:]

    a, b = _lru_coeffs(xc, p, w2_ref)
    a, b = _group_scan(a, b)
    h = hc_s[0:1, :]
    hs = []
    for g in range(LRU_CHUNK // SUBLANES):
        sl = slice(g * SUBLANES, (g + 1) * SUBLANES)
        hg = a[sl] * h + b[sl]
        hs.append(hg)
        h = hg[SUBLANES - 1:SUBLANES, :]
    hc_s[...] = jnp.broadcast_to(h, (SUBLANES, LRU_WIDTH))
    hlast_ref[0] = jnp.broadcast_to(h, (SUBLANES, LRU_WIDTH))
    y_ref[...] = _lru_finish(jnp.concatenate(hs, axis=0), gate_ref[...], p)


def _lru_prompt(u, gate, cs8, h08, lru_p, w2, layer, batch, seq):
    n_chunks = seq // LRU_CHUNK
    n_total = batch * seq
    return pl.pallas_call(
        _lru_prompt_kernel,
        grid=(batch, n_chunks),
        in_specs=[pl.BlockSpec((LRU_CHUNK, LRU_WIDTH), lambda b, c: (b * n_chunks + c, 0)),
                  pl.BlockSpec((LRU_CHUNK, LRU_WIDTH), lambda b, c: (b * n_chunks + c, 0)),
                  pl.BlockSpec((1, SUBLANES, LRU_WIDTH), lambda b, c: (b, 0, 0)),
                  pl.BlockSpec((1, SUBLANES, LRU_WIDTH), lambda b, c: (b, 0, 0)),
                  pl.BlockSpec((1, 16, LRU_WIDTH), lambda b, c: (layer, 0, 0)),
                  pl.BlockSpec((1, 2, 256, 512), lambda b, c: (layer, 0, 0, 0))],
        out_specs=[pl.BlockSpec((LRU_CHUNK, LRU_WIDTH), lambda b, c: (b * n_chunks + c, 0)),
                   pl.BlockSpec((1, SUBLANES, LRU_WIDTH), lambda b, c: (b, 0, 0))],
        out_shape=[jax.ShapeDtypeStruct((n_total, LRU_WIDTH), BF16),
                   jax.ShapeDtypeStruct((batch, SUBLANES, LRU_WIDTH), F32)],
        scratch_shapes=[pltpu.VMEM((SUBLANES, LRU_WIDTH), F32), pltpu.VMEM((SUBLANES, LRU_WIDTH), F32)],
        compiler_params=_params("arbitrary", "arbitrary"),
        name="lru_prompt",
    )(u, gate, cs8, h08, lru_p, w2)


def _lru_sample_kernel(u_ref, gate_ref, cs_ref, h0_ref, p_ref, w2_ref, y_ref, h_ref):
    p = p_ref[0]
    u = u_ref[...]
    cs = cs_ref[...]
    rows = u.shape[0]
    row = lax.broadcasted_iota(I32, u.shape, 0) & (SUBLANES - 1)
    xc = p[4:5, :] + p[3:4, :] * u
    for s in (1, 2, 3):
        sh = pltpu.roll(u, s, 0)
        cs_sh = pltpu.roll(cs, rows - SUBLANES + s, 0)
        xc = xc + p[3 - s:4 - s, :] * jnp.where(row < s, cs_sh, sh)
    a, b = _lru_coeffs(xc, p, w2_ref)
    a, b = _group_scan(a, b)
    h = a * h0_ref[...] + b
    h_ref[...] = h
    y_ref[...] = _lru_finish(h, gate_ref[...], p)


def _lru_sample(u, gate, cs_rows, h0_rows, lru_p, w2, layer, n_prompt, n_sample):
    tile = min(256, n_sample)
    off = n_prompt // tile
    return pl.pallas_call(
        _lru_sample_kernel,
        grid=(n_sample // tile,),
        in_specs=[pl.BlockSpec((tile, LRU_WIDTH), lambda i: (off + i, 0)),
                  pl.BlockSpec((tile, LRU_WIDTH), lambda i: (off + i, 0)),
                  pl.BlockSpec((tile, LRU_WIDTH), lambda i: (i, 0)),
                  pl.BlockSpec((tile, LRU_WIDTH), lambda i: (i, 0)),
                  pl.BlockSpec((1, 16, LRU_WIDTH), lambda i: (layer, 0, 0)),
                  pl.BlockSpec((1, 2, 256, 512), lambda i: (layer, 0, 0, 0))],
        out_specs=[pl.BlockSpec((tile, LRU_WIDTH), lambda i: (i, 0)),
                   pl.BlockSpec((tile, LRU_WIDTH), lambda i: (i, 0))],
        out_shape=[jax.ShapeDtypeStruct((n_sample, LRU_WIDTH), BF16),
                   jax.ShapeDtypeStruct((n_sample, LRU_WIDTH), F32)],
        compiler_params=_params("parallel"),
        name="lru_sample",
    )(u, gate, cs_rows, h0_rows, lru_p, w2)


def _prompt_bias_tables():
    slopes = _alibi_slopes()
    q = np.arange(ATTN_BLK)[:, None]
    k = np.arange(2 * ATTN_BLK)[None, :]
    steps = q + ATTN_BLK - k
    valid = (steps >= 0) & (steps <= ATTN_BLK)
    steps_f = q - k
    valid_f = (k < ATTN_BLK) & (steps_f >= 0)
    table = np.full((N_HEADS, 2, 2, ATTN_BLK, 2 * ATTN_BLK), -np.inf, np.float32)
    table16 = np.full((N_HEADS, ATTN_BLK, ATTN_BLK), -np.inf, np.float32)
    for h in range(N_HEADS):
        for di, d in enumerate(DILATIONS[:2]):
            table[h, di, 0] = np.where(valid, -slopes[h] * d * steps, -np.inf)
            table[h, di, 1] = np.where(valid_f, -slopes[h] * d * steps_f, -np.inf)
        s16 = q - k[:, :ATTN_BLK]
        table16[h] = np.where(s16 >= 0, -slopes[h] * DILATIONS[2] * s16, -np.inf)
    return table, table16


def _attn_prompt_kernel(q_ref, k_ref, v_ref, bias_ref, bias16_ref, o_ref, acc_s, m_s, l_s, *, seq):
    lane = lax.broadcasted_iota(I32, (ATTN_BLK, LANES), 1)
    lo = lane < HEAD_DIM

    def rows(start, size, stride):
        return pl.ds(start, size) if stride == 1 else pl.ds(start, size, stride=stride)

    def tile(di, qstart, kstart, nk, stride, bias_of_head):
        qsl = rows(qstart, ATTN_BLK, stride)
        ksl = rows(kstart, nk, stride)
        qs = q_ref[qsl, :] * ATTN_SCALE
        kk = k_ref[ksl, :].astype(BF16)
        vv = v_ref[ksl, :].astype(BF16)
        parts = []
        for hh in (0, 1):
            qm = jnp.where(lo if hh == 0 else jnp.logical_not(lo), qs, 0.0).astype(BF16)
            s = lax.dot_general(qm, kk, (((1,), (1,)), ((), ())), preferred_element_type=F32)
            s = s + bias_of_head(hh)
            m = jnp.max(s, axis=-1, keepdims=True)
            e = jnp.exp(s - m)
            l = jnp.sum(e, axis=-1, keepdims=True)
            pv = jnp.dot(e.astype(BF16), vv, preferred_element_type=F32)
            parts.append((pv, m, l))
        (pv0, m0, l0), (pv1, m1, l1) = parts
        acc_s[di, qsl, :] = jnp.where(lo, pv0, pv1)
        m_s[di, qsl, :] = jnp.where(lo, m0, m1)
        l_s[di, qsl, :] = jnp.where(lo, l0, l1)

    for di, d in enumerate(DILATIONS[:2]):
        n_tiles = seq // (d * ATTN_BLK)
        span = d * ATTN_BLK

        def body(idx, carry, di=di, d=d, n_tiles=n_tiles, span=span):
            r = idx // n_tiles
            j = idx - r * n_tiles
            first = 1 - jnp.minimum(j, 1)
            qstart = r + span * j
            kstart = r + span * jnp.maximum(j - 1, 0)
            if d == 1:
                qstart = pl.multiple_of(span * j, ATTN_BLK)
                kstart = pl.multiple_of(span * jnp.maximum(j - 1, 0), ATTN_BLK)
            tile(di, qstart, kstart, 2 * ATTN_BLK, d, lambda hh: bias_ref[hh, di, first])
            return carry

        lax.fori_loop(0, d * n_tiles, body, 0)

    d16 = DILATIONS[2]

    def body16(r, carry):
        tile(2, r, r, ATTN_BLK, d16, lambda hh: bias16_ref[hh])
        return carry

    lax.fori_loop(0, d16, body16, 0)

    chunk = 256

    def merge(ci, carry):
        sl = pl.ds(pl.multiple_of(ci * chunk, chunk), chunk)
        m0, m1, m2 = m_s[0, sl, :], m_s[1, sl, :], m_s[2, sl, :]
        mm = jnp.maximum(jnp.maximum(m0, m1), m2)
        w0, w1, w2 = jnp.exp(m0 - mm), jnp.exp(m1 - mm), jnp.exp(m2 - mm)
        den = w0 * l_s[0, sl, :] + w1 * l_s[1, sl, :] + w2 * l_s[2, sl, :]
        num = w0 * acc_s[0, sl, :] + w1 * acc_s[1, sl, :] + w2 * acc_s[2, sl, :]
        o_ref[sl, :] = num / den
        return carry

    lax.fori_loop(0, seq // chunk, merge, 0)


def _attn_prompt(q, k, v, bias, bias16, batch, seq):
    assert seq == DILATIONS[2] * ATTN_BLK, "dilation-16 pass assumes one tile per residue class"
    hp = N_HEADS // 2
    n_total = batch * seq
    qkv_spec = pl.BlockSpec((seq, LANES), lambda b, p: (b, p))
    return pl.pallas_call(
        functools.partial(_attn_prompt_kernel, seq=seq),
        grid=(batch, hp),
        in_specs=[qkv_spec, qkv_spec, qkv_spec,
                  pl.BlockSpec((2, 2, 2, ATTN_BLK, 2 * ATTN_BLK), lambda b, p: (p, 0, 0, 0, 0)),
                  pl.BlockSpec((2, ATTN_BLK, ATTN_BLK), lambda b, p: (p, 0, 0))],
        out_specs=pl.BlockSpec((seq, LANES), lambda b, p: (b, p)),
        out_shape=jax.ShapeDtypeStruct((n_total, ATTN_WIDTH), F32),
        scratch_shapes=[pltpu.VMEM((3, seq, LANES), F32)] * 3,
        compiler_params=_params("parallel", "parallel"),
        name="attn_prompt",
    )(q, k, v, bias, bias16)


def _sample_bias_tables(w_buf, dec_seq):
    slopes = np.asarray(_alibi_slopes())
    rows = np.arange(N_HEADS * dec_seq)
    h, t = rows // dec_seq, rows % dec_seq
    dist = w_buf + t[:, None] - np.arange(w_buf)[None, :]
    cnt = np.zeros(dist.shape, np.float32)
    for d in DILATIONS:
        cnt += ((dist % d == 0) & (dist <= d * ATTN_BLK) & (dist >= 0))
    bias = np.where(cnt > 0, -slopes[h][:, None] * dist, -np.inf).astype(np.float32)
    tn = np.arange(LANES)[None, :]
    dn = t[:, None] - tn
    valid_n = (dn >= 0) & (tn < dec_seq)
    cnt_n = np.zeros(dn.shape, np.float32)
    for d in DILATIONS:
        cnt_n += (valid_n & (dn % d == 0) & (dn <= d * ATTN_BLK))
    bias_n = np.where(cnt_n > 0, -slopes[h][:, None] * dn, -np.inf).astype(np.float32)
    return bias, cnt, bias_n, cnt_n


def _attn_sample_kernel(q_ref, k_ref, v_ref, ck_ref, cv_ref, bias_ref, cnt_ref, biasn_ref, cntn_ref,
                        o_ref, *, dec_seq):
    n_rows = N_HEADS * dec_seq
    q = q_ref[...] * ATTN_SCALE
    qt = jnp.concatenate([q] * N_HEADS, axis=0)
    own = (lax.broadcasted_iota(I32, (n_rows, ATTN_WIDTH), 0) // dec_seq
           == lax.broadcasted_iota(I32, (n_rows, ATTN_WIDTH), 1) // HEAD_DIM)
    qm = jnp.where(own, qt, 0.0).astype(BF16)
    nt = (((1,), (1,)), ((), ()))
    kc = ck_ref[0, 0].astype(BF16)
    s_c = lax.dot_general(qm, kc, nt, preferred_element_type=F32) + bias_ref[...]
    pad = jnp.zeros((LANES - dec_seq, ATTN_WIDTH), F32)
    kn = jnp.concatenate([k_ref[...], pad], axis=0).astype(BF16)
    vn = jnp.concatenate([v_ref[...], pad], axis=0).astype(BF16)
    s_n = lax.dot_general(qm, kn, nt, preferred_element_type=F32) + biasn_ref[...]
    m = jnp.maximum(jnp.max(s_c, axis=-1, keepdims=True), jnp.max(s_n, axis=-1, keepdims=True))
    e_c = jnp.exp(s_c - m) * cnt_ref[...]
    e_n = jnp.exp(s_n - m) * cntn_ref[...]
    l = jnp.sum(e_c, axis=-1, keepdims=True) + jnp.sum(e_n, axis=-1, keepdims=True)
    vc = cv_ref[0, 0].astype(BF16)
    r = (jnp.dot(e_c.astype(BF16), vc, preferred_element_type=F32)
         + jnp.dot(e_n.astype(BF16), vn, preferred_element_type=F32))
    r = jnp.where(own, r, 0.0) / l
    o = r[0:dec_seq]
    for h in range(1, N_HEADS):
        o = o + r[h * dec_seq:(h + 1) * dec_seq]
    o_ref[...] = o


def _attn_sample(q, k, v, cache_k, cache_v, tables, layer, n_prompt, dec_batch, dec_seq):
    w_buf = cache_k.shape[2]
    off = n_prompt // dec_seq
    n_rows = N_HEADS * dec_seq
    row_spec = pl.BlockSpec((dec_seq, ATTN_WIDTH), lambda b: (off + b, 0))
    cache_spec = pl.BlockSpec((1, 1, w_buf, ATTN_WIDTH), lambda b: (layer, b, 0, 0))
    const = lambda shape: pl.BlockSpec(shape, lambda b: (0, 0))
    return pl.pallas_call(
        functools.partial(_attn_sample_kernel, dec_seq=dec_seq),
        grid=(dec_batch,),
        in_specs=[row_spec, row_spec, row_spec, cache_spec, cache_spec,
                  const((n_rows, w_buf)), const((n_rows, w_buf)),
                  const((n_rows, LANES)), const((n_rows, LANES))],
        out_specs=pl.BlockSpec((dec_seq, ATTN_WIDTH), lambda b: (b, 0)),
        out_shape=jax.ShapeDtypeStruct((dec_batch * dec_seq, ATTN_WIDTH), F32),
        compiler_params=_params("parallel"),
        name="attn_sample",
    )(q, k, v, cache_k, cache_v, *tables)


def _layer_norm(z, g, b):
    mu = jnp.mean(z, axis=-1, keepdims=True)
    zc = z - mu
    var = jnp.mean(zc * zc, axis=-1, keepdims=True)
    return zc * lax.rsqrt(var + LN_EPS) * g + b


def _mix_route_kernel(x_ref, yl_ref, oa_ref, wout_ref, vec_ref, wr_ref, br_ref, tri_ref,
                      x1_ref, idx_ref, gate_ref, rank_ref, cnt_ref, carry_s):
    i = pl.program_id(0)

    @pl.when(i == 0)
    def _():
        carry_s[...] = jnp.zeros_like(carry_s)

    vec = vec_ref[0]
    oa = oa_ref[...]
    oa_n = (oa * lax.rsqrt(jnp.mean(oa * oa, axis=-1, keepdims=True) + RMS_EPS) * vec[0:1, :ATTN_WIDTH])
    mixed = (jnp.dot(yl_ref[...], wout_ref[0, :LRU_WIDTH], preferred_element_type=F32)
             + jnp.dot(oa_n.astype(BF16), wout_ref[0, LRU_WIDTH:], preferred_element_type=F32))
    x1 = _layer_norm(DEEPNORM_ALPHA * x_ref[...] + mixed, vec[1:2, :], vec[2:3, :])
    x1_ref[...] = x1

    logits = lax.dot_general(wr_ref[0], x1.astype(BF16), (((1,), (1,)), ((), ())),
                             preferred_element_type=F32) + br_ref[0]
    tile = logits.shape[1]
    e_iota = lax.broadcasted_iota(I32, logits.shape, 0).astype(F32)
    work = logits
    vals, idxs, sels = [], [], []
    for _ in range(TOP_K):
        mx = jnp.max(work, axis=0, keepdims=True)
        idx = jnp.min(jnp.where(work == mx, e_iota, float(N_EXPERTS)), axis=0, keepdims=True)
        sel = e_iota == idx
        vals.append(mx)
        idxs.append(idx)
        sels.append(sel)
        work = jnp.where(sel, -jnp.inf, work)
    exps = [jnp.exp(v - vals[0]) for v in vals]
    den = exps[0] + exps[1] + exps[2] + exps[3]
    gates = [e / den for e in exps]

    onehot = jnp.zeros(logits.shape, F32)
    for sel in sels:
        onehot = onehot + sel.astype(F32)
    prefix = jnp.dot(onehot.astype(BF16), tri_ref[...], preferred_element_type=F32) + carry_s[:, 0:1]
    ranks = [jnp.sum(jnp.where(sel, prefix, 0.0), axis=0, keepdims=True) for sel in sels]
    carry = carry_s[:, 0:1] + jnp.sum(onehot, axis=1, keepdims=True)
    carry_s[...] = jnp.broadcast_to(carry, carry_s.shape)
    cnt_ref[...] = jnp.broadcast_to(carry, cnt_ref.shape)

    row = lax.broadcasted_iota(I32, (SUBLANES, tile), 0)

    def stack(parts):
        out = jnp.zeros((SUBLANES, tile), F32)
        for k_, part in enumerate(parts):
            out = jnp.where(row == k_, part, out)
        return out

    idx_ref[...] = stack(idxs).astype(I32)
    gate_ref[...] = stack(gates)
    rank_ref[...] = stack(ranks).astype(I32)


def _mix_route(x, y_lru, o_attn, w_out_bf, vec1, wr_t, b_router, tri, layer):
    n = x.shape[0]
    t = TOKEN_TILE
    row = lambda w: pl.BlockSpec((t, w), lambda i: (i, 0))
    col = pl.BlockSpec((SUBLANES, t), lambda i: (0, i))
    return pl.pallas_call(
        _mix_route_kernel,
        grid=(n // t,),
        in_specs=[row(D_MODEL), row(LRU_WIDTH), row(ATTN_WIDTH),
                  pl.BlockSpec((1, D_MODEL, D_MODEL), lambda i: (layer, 0, 0)),
                  pl.BlockSpec((1, SUBLANES, D_MODEL), lambda i: (layer, 0, 0)),
                  pl.BlockSpec((1, N_EXPERTS, D_MODEL), lambda i: (layer, 0, 0)),
                  pl.BlockSpec((1, N_EXPERTS, 1), lambda i: (layer, 0, 0)),
                  pl.BlockSpec((t, t), lambda i: (0, 0))],
        out_specs=[row(D_MODEL), col, col, col,
                   pl.BlockSpec((N_EXPERTS, LANES), lambda i: (0, 0))],
        out_shape=[jax.ShapeDtypeStruct((n, D_MODEL), F32),
                   jax.ShapeDtypeStruct((SUBLANES, n), I32),
                   jax.ShapeDtypeStruct((SUBLANES, n), F32),
                   jax.ShapeDtypeStruct((SUBLANES, n), I32),
                   jax.ShapeDtypeStruct((N_EXPERTS, LANES), F32)],
        scratch_shapes=[pltpu.VMEM((N_EXPERTS, LANES), F32)],
        compiler_params=_params("arbitrary"),
        name="mix_route",
    )(x, y_lru, o_attn, w_out_bf, vec1, wr_t, b_router, tri)


def _dispatch_kernel(tok_hbm, x_hbm, xs_ref, idx_s, idx_sem, sem):
    i = pl.program_id(0)
    cp = pltpu.make_async_copy(
        tok_hbm.at[pl.ds(pl.multiple_of(i * DISPATCH_TILE, DISPATCH_TILE), DISPATCH_TILE)], idx_s, idx_sem)
    cp.start()
    cp.wait()

    def row_copy(r):
        return pltpu.make_async_copy(x_hbm.at[pl.ds(idx_s[r], 1), :], xs_ref.at[pl.ds(r, 1), :], sem)

    def issue(r, carry):
        row_copy(r).start()
        return carry

    lax.fori_loop(0, DISPATCH_TILE, issue, 0)

    def drain(r, carry):
        row_copy(r).wait()
        return carry

    lax.fori_loop(0, DISPATCH_TILE, drain, 0)


def _dispatch(row_token, x1, n_rows):
    return pl.pallas_call(
        _dispatch_kernel,
        grid=(n_rows // DISPATCH_TILE,),
        in_specs=[pl.BlockSpec(memory_space=pl.ANY), pl.BlockSpec(memory_space=pl.ANY)],
        out_specs=pl.BlockSpec((DISPATCH_TILE, D_MODEL), lambda i: (i, 0)),
        out_shape=jax.ShapeDtypeStruct((n_rows, D_MODEL), F32),
        scratch_shapes=[pltpu.SMEM((DISPATCH_TILE,), I32),
                        pltpu.SemaphoreType.DMA(()), pltpu.SemaphoreType.DMA(())],
        compiler_params=_params("arbitrary"),
        name="dispatch",
    )(row_token, x1)


def _expert_kernel(te_ref, tf_ref, tv_ref, xs_ref, wu_ref, bu_ref, wd_ref, bd_ref, ys_ref, wub_s, wdb_s):
    del te_ref
    i = pl.program_id(0)

    @pl.when(tf_ref[i] == 1)
    def _():
        wub_s[...] = wu_ref[0, 0].astype(BF16)
        wdb_s[...] = wd_ref[0, 0].astype(BF16)

    @pl.when(tv_ref[i] == 0)
    def _():
        ys_ref[...] = jnp.zeros_like(ys_ref)

    @pl.when(tv_ref[i] == 1)
    def _():
        xb = xs_ref[...].astype(BF16)
        h = jnp.dot(xb, wub_s[...], preferred_element_type=F32) + bu_ref[0, 0]
        glu = jnp.minimum(h[:, :D_FF], SWIGLU_LIMIT)
        lin = jnp.clip(h[:, D_FF:], -SWIGLU_LIMIT, SWIGLU_LIMIT)
        act = glu * jax.nn.sigmoid(SWIGLU_ALPHA * glu) * (lin + 1.0)
        ys_ref[...] = jnp.dot(act.astype(BF16), wdb_s[...], preferred_element_type=F32) + bd_ref[0, 0]


def _experts(tile_expert, tile_first, tile_valid, xs, w_up, b_up4, w_down, b_down4, layer):
    n_rows = xs.shape[0]
    n_tiles = n_rows // EXPERT_TILE
    grid_spec = pltpu.PrefetchScalarGridSpec(
        num_scalar_prefetch=3,
        grid=(n_tiles,),
        in_specs=[pl.BlockSpec((EXPERT_TILE, D_MODEL), lambda i, te, tf, tv: (i, 0)),
                  pl.BlockSpec((1, 1, D_MODEL, 2 * D_FF), lambda i, te, tf, tv: (layer, te[i], 0, 0)),
                  pl.BlockSpec((1, 1, 1, 2 * D_FF), lambda i, te, tf, tv: (layer, te[i], 0, 0)),
                  pl.BlockSpec((1, 1, D_FF, D_MODEL), lambda i, te, tf, tv: (layer, te[i], 0, 0)),
                  pl.BlockSpec((1, 1, 1, D_MODEL), lambda i, te, tf, tv: (layer, te[i], 0, 0))],
        out_specs=pl.BlockSpec((EXPERT_TILE, D_MODEL), lambda i, te, tf, tv: (i, 0)),
        scratch_shapes=[pltpu.VMEM((D_MODEL, 2 * D_FF), BF16), pltpu.VMEM((D_FF, D_MODEL), BF16)],
    )
    return pl.pallas_call(
        _expert_kernel,
        grid_spec=grid_spec,
        out_shape=jax.ShapeDtypeStruct((n_rows, D_MODEL), F32),
        compiler_params=_params("arbitrary"),
        name="experts",
    )(tile_expert, tile_first, tile_valid, xs, w_up, b_up4, w_down, b_down4)


def _combine_kernel(dest_hbm, ys_hbm, x1_ref, gate_ref, ple_ref, vec_ref, wg_ref, wp_ref, o_ref,
                    idx_s, buf_s, idx_sem, sem):
    i = pl.program_id(0)
    n_idx = TOP_K * COMBINE_TILE
    cp = pltpu.make_async_copy(dest_hbm.at[pl.ds(pl.multiple_of(i * n_idx, n_idx), n_idx)], idx_s, idx_sem)
    cp.start()
    cp.wait()

    def row_copy(t, k):
        return pltpu.make_async_copy(ys_hbm.at[pl.ds(idx_s[t * TOP_K + k], 1), :],
                                     buf_s.at[k, pl.ds(t, 1), :], sem)

    def issue(t, carry):
        for k in range(TOP_K):
            row_copy(t, k).start()
        return carry

    lax.fori_loop(0, COMBINE_TILE, issue, 0)

    def drain(t, carry):
        for k in range(TOP_K):
            row_copy(t, k).wait()
        return carry

    lax.fori_loop(0, COMBINE_TILE, drain, 0)

    vec = vec_ref[0]
    g = gate_ref[...]
    ffn = g[:, 0:1] * buf_s[0]
    for k in range(1, TOP_K):
        ffn = ffn + g[:, k:k + 1] * buf_s[k]
    x2 = _layer_norm(DEEPNORM_ALPHA * x1_ref[...] + ffn, vec[3:4, :], vec[4:5, :])
    pg = jax.nn.sigmoid(jnp.dot(x2.astype(BF16), wg_ref[0], preferred_element_type=F32))
    pp = jnp.dot(ple_ref[...].astype(BF16), wp_ref[0], preferred_element_type=F32)
    o_ref[...] = x2 + pg * pp


def _combine(dest_flat, ys, x1, gates_col, ple, vec1, wg_bf, wp_bf, layer):
    n = x1.shape[0]
    t = COMBINE_TILE
    return pl.pallas_call(
        _combine_kernel,
        grid=(n // t,),
        in_specs=[pl.BlockSpec(memory_space=pl.ANY),
                  pl.BlockSpec(memory_space=pl.ANY),
                  pl.BlockSpec((t, D_MODEL), lambda i: (i, 0)),
                  pl.BlockSpec((t, TOP_K), lambda i: (i, 0)),
                  pl.BlockSpec((t, PLE_DIM), lambda i: (i, 0)),
                  pl.BlockSpec((1, SUBLANES, D_MODEL), lambda i: (layer, 0, 0)),
                  pl.BlockSpec((1, D_MODEL, D_MODEL), lambda i: (layer, 0, 0)),
                  pl.BlockSpec((1, PLE_DIM, D_MODEL), lambda i: (layer, 0, 0))],
        out_specs=pl.BlockSpec((t, D_MODEL), lambda i: (i, 0)),
        out_shape=jax.ShapeDtypeStruct((n, D_MODEL), F32),
        scratch_shapes=[pltpu.SMEM((TOP_K * t,), I32),
                        pltpu.VMEM((TOP_K, t, D_MODEL), F32),
                        pltpu.SemaphoreType.DMA(()), pltpu.SemaphoreType.DMA(())],
        compiler_params=_params("arbitrary"),
        name="combine",
    )(dest_flat, ys, x1, gates_col, ple, vec1, wg_bf, wp_bf)


def _block_diag_pairs(w_a, w_x):
    depth = w_a.shape[0]
    eye = jnp.eye(4, dtype=w_a.dtype)

    def halves(w):
        w = w.reshape(depth, 2, 4, LRU_BLOCK_DIM, LRU_BLOCK_DIM)
        return jnp.einsum('lhnij,nm->lhnimj', w, eye).reshape(depth, 2, 256, 256)

    return jnp.concatenate([halves(w_a), halves(w_x)], axis=-1).astype(BF16)


def kernel(x_prompt, x_sample, cache_k, cache_v, state_conv, state_h, p_prompt, p_sample, w_in, conv_w, conv_b,
           w_gate_a, b_gate_a, w_gate_x, b_gate_x, lru_lambda, g_lru_norm, g_attn_norm, w_out, ln1_g, ln1_b,
           w_router, b_router, w_up, b_up, w_down, b_down, ln2_g, ln2_b, w_ple_proj, w_ple_gate):
    depth = w_in.shape[0]
    batch, seq, _ = x_prompt.shape
    dec_batch, dec_seq, _ = x_sample.shape
    w_buf = cache_k.shape[2]
    n_prompt, n_sample = batch * seq, dec_batch * dec_seq
    n = n_prompt + n_sample
    assert dec_seq == SUBLANES and seq % LRU_CHUNK == 0 and n % TOKEN_TILE == 0 and n_prompt % TOKEN_TILE == 0

    w_in_bf = w_in.astype(BF16)
    w_out_bf = w_out.astype(BF16)
    wg_bf = w_ple_gate.astype(BF16)
    wp_bf = w_ple_proj.astype(BF16)
    wr_t = jnp.swapaxes(w_router, 1, 2).astype(BF16)
    b_router3 = b_router[:, :, None]
    w2 = _block_diag_pairs(w_gate_a, w_gate_x)
    zeros512 = jnp.zeros((depth, 7, LRU_WIDTH), F32)
    lru_p = jnp.concatenate([conv_w, conv_b[:, None], b_gate_a[:, None], b_gate_x[:, None], lru_lambda[:, None],
                             g_lru_norm[:, None], zeros512], axis=1)
    pad_attn = jnp.zeros((depth, D_MODEL - ATTN_WIDTH), F32)
    vec1 = jnp.stack([jnp.concatenate([g_attn_norm, pad_attn], axis=1), ln1_g, ln1_b, ln2_g, ln2_b,
                      jnp.zeros_like(ln1_g), jnp.zeros_like(ln1_g), jnp.zeros_like(ln1_g)], axis=1)
    b_up4 = b_up[:, :, None, :]
    b_down4 = b_down[:, :, None, :]
    tri = jnp.asarray(np.triu(np.ones((TOKEN_TILE, TOKEN_TILE), np.float32), k=1), BF16)
    bias_np, bias16_np = _prompt_bias_tables()
    bias_p, bias16_p = jnp.asarray(bias_np), jnp.asarray(bias16_np)
    sample_tables = tuple(jnp.asarray(t) for t in _sample_bias_tables(w_buf, dec_seq))

    cache_k4 = cache_k.reshape(depth, dec_batch, w_buf, ATTN_WIDTH)
    cache_v4 = cache_v.reshape(depth, dec_batch, w_buf, ATTN_WIDTH)
    cs_rows = jnp.pad(state_conv, ((0, 0), (0, 0), (SUBLANES - (CONV_WIDTH - 1), 0), (0, 0))
                      ).reshape(depth, n_sample, LRU_WIDTH)
    h0_rows = jnp.repeat(state_h, dec_seq, axis=1)
    cs_zero = jnp.zeros((batch, SUBLANES, LRU_WIDTH), F32)
    ple_all = jnp.concatenate([p_prompt.reshape(depth, n_prompt, PLE_DIM),
                               p_sample.reshape(depth, n_sample, PLE_DIM)], axis=1)

    n_tiles = (n * TOP_K) // EXPERT_TILE + N_EXPERTS
    n_rows = n_tiles * EXPERT_TILE
    assert n_rows % DISPATCH_TILE == 0
    tile_start = jnp.arange(n_tiles, dtype=I32) * EXPERT_TILE
    token_ids = jnp.arange(n * TOP_K, dtype=I32) // TOP_K

    x = jnp.concatenate([x_prompt.reshape(n_prompt, D_MODEL), x_sample.reshape(n_sample, D_MODEL)], axis=0)
    kp, vp, cp, hp, ks, vs, cs, hs = [], [], [], [], [], [], [], []
    for layer in range(depth):
        u, gate, q, k, v = _inproj(x, w_in_bf, layer)

        y_lru_p, h_last = _lru_prompt(u, gate, cs_zero, cs_zero, lru_p, w2, layer, batch, seq)
        y_lru_s, h_all = _lru_sample(u, gate, cs_rows[layer], h0_rows[layer], lru_p, w2,
                                     layer, n_prompt, n_sample)
        y_lru = jnp.concatenate([y_lru_p, y_lru_s], axis=0)

        o_attn_p = _attn_prompt(q, k, v, bias_p, bias16_p, batch, seq)
        o_attn_s = _attn_sample(q, k, v, cache_k4, cache_v4, sample_tables,
                                layer, n_prompt, dec_batch, dec_seq)
        o_attn = jnp.concatenate([o_attn_p, o_attn_s], axis=0)

        x1, idx_t, gate_t, rank_t, cnt = _mix_route(x, y_lru, o_attn, w_out_bf, vec1, wr_t, b_router3, tri, layer)

        counts = cnt[:, 0].astype(I32)
        padded = (counts + EXPERT_TILE - 1) // EXPERT_TILE * EXPERT_TILE
        ends = jnp.cumsum(padded)
        starts = ends - padded
        dest = starts[idx_t[:TOP_K]] + rank_t[:TOP_K]
        dest_flat = dest.T.reshape(-1)
        tile_expert = jnp.minimum(jnp.searchsorted(ends, tile_start, side='right'), N_EXPERTS - 1).astype(I32)
        tile_valid = (tile_start < ends[-1]).astype(I32)
        tile_expert = jnp.where(tile_valid == 1, tile_expert,
                                jnp.max(jnp.where(tile_valid == 1, tile_expert, 0)))
        changed = jnp.concatenate([jnp.ones((1,), I32), (tile_expert[1:] != tile_expert[:-1]).astype(I32)])
        tile_first = changed * tile_valid

        row_token = jnp.zeros((n_rows,), I32).at[dest_flat].set(token_ids)
        xs = _dispatch(row_token, x1, n_rows)
        ys = _experts(tile_expert, tile_first, tile_valid, xs, w_up, b_up4, w_down, b_down4, layer)
        x = _combine(dest_flat, ys, x1, gate_t[:TOP_K].T, ple_all[layer], vec1, wg_bf, wp_bf, layer)

        k_p = k[:n_prompt].reshape(batch, seq, N_HEADS, HEAD_DIM)
        v_p = v[:n_prompt].reshape(batch, seq, N_HEADS, HEAD_DIM)
        w_prompt = min(DILATIONS[2] * ATTN_BLK, seq)
        kp.append(k_p[:, -w_prompt:])
        vp.append(v_p[:, -w_prompt:])
        cp.append(u[:n_prompt].reshape(batch, seq, LRU_WIDTH)[:, seq - (CONV_WIDTH - 1):])
        hp.append(h_last[:, 0])
        ks.append(k[n_prompt:].reshape(dec_batch, dec_seq, N_HEADS, HEAD_DIM))
        vs.append(v[n_prompt:].reshape(dec_batch, dec_seq, N_HEADS, HEAD_DIM))
        cs.append(u[n_prompt:].reshape(dec_batch, dec_seq, LRU_WIDTH)[:, dec_seq - (CONV_WIDTH - 1):])
        hs.append(h_all.reshape(dec_batch, dec_seq, LRU_WIDTH)[:, dec_seq - 1])

    y_prompt = x[:n_prompt].reshape(batch, seq, D_MODEL)
    y_sample = x[n_prompt:].reshape(dec_batch, dec_seq, D_MODEL)
    return (y_prompt, y_sample, jnp.stack(kp), jnp.stack(vp), jnp.stack(cp), jnp.stack(hp),
            jnp.stack(ks), jnp.stack(vs), jnp.stack(cs), jnp.stack(hs))
```

```python
import functools

import numpy as np
import jax
import jax.numpy as jnp
from jax import lax
from jax.experimental import pallas as pl
from jax.experimental.pallas import tpu as pltpu

F32 = jnp.float32
BF16 = jnp.bfloat16
I32 = jnp.int32

D_MODEL = 1024
LRU_WIDTH = 512
LRU_BLOCK_DIM = 64
CONV_WIDTH = 4
LRU_C = 8.0
N_HEADS = 8
HEAD_DIM = 64
ATTN_WIDTH = N_HEADS * HEAD_DIM
ATTN_SCALE = HEAD_DIM ** -0.5
ATTN_BLK = 128
DILATIONS = (1, 4, 16)
N_EXPERTS = 32
TOP_K = 4
D_FF = 1024
SWIGLU_ALPHA = 1.702
SWIGLU_LIMIT = 7.0
PLE_DIM = 256
MODEL_DEPTH = 4
DEEPNORM_ALPHA = (2.0 * MODEL_DEPTH) ** 0.25
LN_EPS = 1e-5
RMS_EPS = 1e-6

SUBLANES = 8
LANES = 128
VMEM_LIMIT_BYTES = 56 * 1024 * 1024

TOKEN_TILE = 512
LRU_CHUNK = 256
EXPERT_TILE = 512
COMBINE_TILE = 512
DISPATCH_TILE = 1024
SAMPLE_NEAR = DILATIONS[1] * ATTN_BLK
ATTN_UNROLL = 4
DMA_UNROLL = 8


def _params(*sem):
    return pltpu.CompilerParams(dimension_semantics=sem, vmem_limit_bytes=VMEM_LIMIT_BYTES)


def _alibi_slopes():
    return [2.0 ** (-8.0 * (h + 1) / N_HEADS) for h in range(N_HEADS)]


def _inproj_kernel(x_ref, w_ref, u_ref, g_ref, q_ref, k_ref, v_ref):
    xb = x_ref[...].astype(BF16)
    for i, o_ref in enumerate((u_ref, g_ref, q_ref, k_ref, v_ref)):
        o_ref[...] = jnp.dot(xb, w_ref[0, :, i * 512:(i + 1) * 512], preferred_element_type=F32)


def _inproj(x, w_in_bf, layer):
    n = x.shape[0]
    out = jax.ShapeDtypeStruct((n, 512), F32)
    return pl.pallas_call(
        _inproj_kernel,
        grid=(n // TOKEN_TILE,),
        in_specs=[pl.BlockSpec((TOKEN_TILE, D_MODEL), lambda i: (i, 0)),
                  pl.BlockSpec((1, D_MODEL, 2560), lambda i: (layer, 0, 0))],
        out_specs=[pl.BlockSpec((TOKEN_TILE, 512), lambda i: (i, 0))] * 5,
        out_shape=[out] * 5,
        compiler_params=_params("parallel"),
        name="inproj",
    )(x, w_in_bf)


def _lru_coeffs(xc, p, w2_ref):
    xb = xc.astype(BF16)
    g0 = jnp.dot(xb[:, :256], w2_ref[0, 0], preferred_element_type=F32)
    g1 = jnp.dot(xb[:, 256:], w2_ref[0, 1], preferred_element_type=F32)
    ga = jnp.concatenate([g0[:, :256], g1[:, :256]], axis=1) + p[5:6, :]
    gx = jnp.concatenate([g0[:, 256:], g1[:, 256:]], axis=1) + p[6:7, :]
    r = jax.nn.sigmoid(ga)
    i = jax.nn.sigmoid(gx)
    z = -p[7:8, :]
    softplus = jnp.maximum(z, 0.0) + jnp.log1p(jnp.exp(-jnp.abs(z)))
    log_a = (-LRU_C) * r * softplus
    a = jnp.exp(log_a)
    b = jnp.sqrt(-jnp.tanh(log_a) * (a * a + 1.0)) * i * xc
    return a, b


def _group_scan(a, b):
    row = lax.broadcasted_iota(I32, a.shape, 0) & (SUBLANES - 1)
    for s in (1, 2, 4):
        a_prev = pltpu.roll(a, s, 0)
        b_prev = pltpu.roll(b, s, 0)
        m = row >= s
        b = jnp.where(m, a * b_prev + b, b)
        a = jnp.where(m, a * a_prev, a)
    return a, b


def _lru_finish(h, gate, p):
    y = h * jax.nn.gelu(gate)
    var = jnp.mean(y * y, axis=-1, keepdims=True)
    return (y * lax.rsqrt(var + RMS_EPS) * p[8:9, :]).astype(BF16)


def _lru_prompt_kernel(u_ref, gate_ref, cs_ref, h0_ref, p_ref, w2_ref, y_ref, hlast_ref, tail_s, hc_s):
    c = pl.program_id(1)

    @pl.when(c == 0)
    def _():
        tail_s[...] = cs_ref[0]
        hc_s[...] = h0_ref[0]

    p = p_ref[0]
    u = u_ref[...]
    tail = tail_s[...]
    row8 = lax.broadcasted_iota(I32, (SUBLANES, LRU_WIDTH), 0)
    xc = p[4:5, :] + p[3:4, :] * u
    for s in (1, 2, 3):
        sh = pltpu.roll(u, s, 0)
        first = jnp.where(row8 < s, pltpu.roll(tail, s, 0), sh[0:SUBLANES])
        xc = xc + p[3 - s:4 - s, :] * jnp.concatenate([first, sh[SUBLANES:]], axis=0)
    tail_s[...] = u[LRU_CHUNK - SUBLANES:]

    a, b = _lru_coeffs(xc, p, w2_ref)
    a, b = _group_scan(a, b)
    h = hc_s[0:1, :]
    hs = []
    for g in range(LRU_CHUNK // SUBLANES):
        sl = slice(g * SUBLANES, (g + 1) * SUBLANES)
        hg = a[sl] * h + b[sl]
        hs.append(hg)
        h = hg[SUBLANES - 1:SUBLANES, :]
    hc_s[...] = jnp.broadcast_to(h, (SUBLANES, LRU_WIDTH))
    hlast_ref[0] = jnp.broadcast_to(h, (SUBLANES, LRU_WIDTH))
    y_ref[...] = _lru_finish(jnp.concatenate(hs, axis=0), gate_ref[...], p)


def _lru_prompt(u, gate, cs8, h08, lru_p, w2, layer, batch, seq):
    n_chunks = seq // LRU_CHUNK
    n_total = batch * seq
    return pl.pallas_call(
        _lru_prompt_kernel,
        grid=(batch, n_chunks),
        in_specs=[pl.BlockSpec((LRU_CHUNK, LRU_WIDTH), lambda b, c: (b * n_chunks + c, 0)),
                  pl.BlockSpec((LRU_CHUNK, LRU_WIDTH), lambda b, c: (b * n_chunks + c, 0)),
                  pl.BlockSpec((1, SUBLANES, LRU_WIDTH), lambda b, c: (b, 0, 0)),
                  pl.BlockSpec((1, SUBLANES, LRU_WIDTH), lambda b, c: (b, 0, 0)),
                  pl.BlockSpec((1, 16, LRU_WIDTH), lambda b, c: (layer, 0, 0)),
                  pl.BlockSpec((1, 2, 256, 512), lambda b, c: (layer, 0, 0, 0))],
        out_specs=[pl.BlockSpec((LRU_CHUNK, LRU_WIDTH), lambda b, c: (b * n_chunks + c, 0)),
                   pl.BlockSpec((1, SUBLANES, LRU_WIDTH), lambda b, c: (b, 0, 0))],
        out_shape=[jax.ShapeDtypeStruct((n_total, LRU_WIDTH), BF16),
                   jax.ShapeDtypeStruct((batch, SUBLANES, LRU_WIDTH), F32)],
        scratch_shapes=[pltpu.VMEM((SUBLANES, LRU_WIDTH), F32), pltpu.VMEM((SUBLANES, LRU_WIDTH), F32)],
        compiler_params=_params("arbitrary", "arbitrary"),
        name="lru_prompt",
    )(u, gate, cs8, h08, lru_p, w2)


def _lru_sample_kernel(u_ref, gate_ref, cs_ref, h0_ref, p_ref, w2_ref, y_ref, h_ref):
    p = p_ref[0]
    u = u_ref[...]
    cs = cs_ref[...]
    rows = u.shape[0]
    row = lax.broadcasted_iota(I32, u.shape, 0) & (SUBLANES - 1)
    xc = p[4:5, :] + p[3:4, :] * u
    for s in (1, 2, 3):
        sh = pltpu.roll(u, s, 0)
        cs_sh = pltpu.roll(cs, rows - SUBLANES + s, 0)
        xc = xc + p[3 - s:4 - s, :] * jnp.where(row < s, cs_sh, sh)
    a, b = _lru_coeffs(xc, p, w2_ref)
    a, b = _group_scan(a, b)
    h = a * h0_ref[...] + b
    h_ref[...] = h
    y_ref[...] = _lru_finish(h, gate_ref[...], p)


def _lru_sample(u, gate, cs_rows, h0_rows, lru_p, w2, layer, n_prompt, n_sample):
    tile = min(256, n_sample)
    off = n_prompt // tile
    return pl.pallas_call(
        _lru_sample_kernel,
        grid=(n_sample // tile,),
        in_specs=[pl.BlockSpec((tile, LRU_WIDTH), lambda i: (off + i, 0)),
                  pl.BlockSpec((tile, LRU_WIDTH), lambda i: (off + i, 0)),
                  pl.BlockSpec((tile, LRU_WIDTH), lambda i: (i, 0)),
                  pl.BlockSpec((tile, LRU_WIDTH), lambda i: (i, 0)),
                  pl.BlockSpec((1, 16, LRU_WIDTH), lambda i: (layer, 0, 0)),
                  pl.BlockSpec((1, 2, 256, 512), lambda i: (layer, 0, 0, 0))],
        out_specs=[pl.BlockSpec((tile, LRU_WIDTH), lambda i: (i, 0)),
                   pl.BlockSpec((tile, LRU_WIDTH), lambda i: (i, 0))],
        out_shape=[jax.ShapeDtypeStruct((n_sample, LRU_WIDTH), BF16),
                   jax.ShapeDtypeStruct((n_sample, LRU_WIDTH), F32)],
        compiler_params=_params("parallel"),
        name="lru_sample",
    )(u, gate, cs_rows, h0_rows, lru_p, w2)


def _prompt_bias_tables():
    slopes = _alibi_slopes()
    q = np.arange(ATTN_BLK)[:, None]
    k = np.arange(2 * ATTN_BLK)[None, :]
    steps = q + ATTN_BLK - k
    valid = (steps >= 0) & (steps <= ATTN_BLK)
    steps_f = q - k
    valid_f = (k < ATTN_BLK) & (steps_f >= 0)
    table = np.full((N_HEADS, 2, 2, ATTN_BLK, 2 * ATTN_BLK), -np.inf, np.float32)
    table16 = np.full((N_HEADS, ATTN_BLK, ATTN_BLK), -np.inf, np.float32)
    for h in range(N_HEADS):
        for di, d in enumerate(DILATIONS[:2]):
            table[h, di, 0] = np.where(valid, -slopes[h] * d * steps, -np.inf)
            table[h, di, 1] = np.where(valid_f, -slopes[h] * d * steps_f, -np.inf)
        s16 = q - k[:, :ATTN_BLK]
        table16[h] = np.where(s16 >= 0, -slopes[h] * DILATIONS[2] * s16, -np.inf)
    return table, table16


def _attn_prompt_kernel(q_ref, k_ref, v_ref, bias_ref, bias16_ref, o_ref, acc_s, m_s, l_s, *, seq):
    lane = lax.broadcasted_iota(I32, (ATTN_BLK, LANES), 1)
    lo = lane < HEAD_DIM

    def rows(start, size, stride):
        return pl.ds(start, size) if stride == 1 else pl.ds(start, size, stride=stride)

    def tile(di, qstart, kstart, nk, stride, bias_of_head):
        qsl = rows(qstart, ATTN_BLK, stride)
        ksl = rows(kstart, nk, stride)
        qs = q_ref[qsl, :] * ATTN_SCALE
        kk = k_ref[ksl, :].astype(BF16)
        vv = v_ref[ksl, :].astype(BF16)
        parts = []
        for hh in (0, 1):
            qm = jnp.where(lo if hh == 0 else jnp.logical_not(lo), qs, 0.0).astype(BF16)
            s = lax.dot_general(qm, kk, (((1,), (1,)), ((), ())), preferred_element_type=F32)
            s = s + bias_of_head(hh)
            m = jnp.max(s, axis=-1, keepdims=True)
            e = jnp.exp(s - m)
            l = jnp.sum(e, axis=-1, keepdims=True)
            pv = jnp.dot(e.astype(BF16), vv, preferred_element_type=F32)
            parts.append((pv, m, l))
        (pv0, m0, l0), (pv1, m1, l1) = parts
        acc_s[di, qsl, :] = jnp.where(lo, pv0, pv1)
        m_s[di, qsl, :] = jnp.where(lo, m0, m1)
        l_s[di, qsl, :] = jnp.where(lo, l0, l1)

    d1, d4, d16 = DILATIONS
    span4 = d4 * ATTN_BLK

    def body1(it, carry):
        for u in range(ATTN_UNROLL):
            j = it * ATTN_UNROLL + u
            qstart = pl.multiple_of(ATTN_BLK * j, ATTN_BLK)
            if u == 0:
                first = 1 - jnp.minimum(j, 1)
                kstart = pl.multiple_of(ATTN_BLK * jnp.maximum(j - 1, 0), ATTN_BLK)
            else:
                first = 0
                kstart = pl.multiple_of(ATTN_BLK * (j - 1), ATTN_BLK)
            tile(0, qstart, kstart, 2 * ATTN_BLK, d1, lambda hh, first=first: bias_ref[hh, 0, first])
        return carry

    lax.fori_loop(0, seq // (ATTN_BLK * ATTN_UNROLL), body1, 0)

    def body4(r, carry):
        for j in range(seq // span4):
            tile(1, r + span4 * j, r + span4 * max(j - 1, 0), 2 * ATTN_BLK, d4,
                 lambda hh, first=int(j == 0): bias_ref[hh, 1, first])
        return carry

    lax.fori_loop(0, d4, body4, 0)

    def body16(it, carry):
        for u in range(ATTN_UNROLL):
            r = it * ATTN_UNROLL + u
            tile(2, r, r, ATTN_BLK, d16, lambda hh: bias16_ref[hh])
        return carry

    lax.fori_loop(0, d16 // ATTN_UNROLL, body16, 0)

    chunk = 256

    def merge(ci, carry):
        sl = pl.ds(pl.multiple_of(ci * chunk, chunk), chunk)
        m0, m1, m2 = m_s[0, sl, :], m_s[1, sl, :], m_s[2, sl, :]
        mm = jnp.maximum(jnp.maximum(m0, m1), m2)
        w0, w1, w2 = jnp.exp(m0 - mm), jnp.exp(m1 - mm), jnp.exp(m2 - mm)
        den = w0 * l_s[0, sl, :] + w1 * l_s[1, sl, :] + w2 * l_s[2, sl, :]
        num = w0 * acc_s[0, sl, :] + w1 * acc_s[1, sl, :] + w2 * acc_s[2, sl, :]
        o_ref[sl, :] = num / den
        return carry

    lax.fori_loop(0, seq // chunk, merge, 0)


def _attn_prompt(q, k, v, bias, bias16, batch, seq):
    assert seq == DILATIONS[2] * ATTN_BLK, "dilation-16 pass assumes one tile per residue class"
    hp = N_HEADS // 2
    n_total = batch * seq
    qkv_spec = pl.BlockSpec((seq, LANES), lambda b, p: (b, p))
    return pl.pallas_call(
        functools.partial(_attn_prompt_kernel, seq=seq),
        grid=(batch, hp),
        in_specs=[qkv_spec, qkv_spec, qkv_spec,
                  pl.BlockSpec((2, 2, 2, ATTN_BLK, 2 * ATTN_BLK), lambda b, p: (p, 0, 0, 0, 0)),
                  pl.BlockSpec((2, ATTN_BLK, ATTN_BLK), lambda b, p: (p, 0, 0))],
        out_specs=pl.BlockSpec((seq, LANES), lambda b, p: (b, p)),
        out_shape=jax.ShapeDtypeStruct((n_total, ATTN_WIDTH), F32),
        scratch_shapes=[pltpu.VMEM((3, seq, LANES), F32)] * 3,
        compiler_params=_params("parallel", "parallel"),
        name="attn_prompt",
    )(q, k, v, bias, bias16)


def _sample_tables(w_buf, dec_seq):
    slopes = np.asarray(_alibi_slopes())
    rows = np.arange(dec_seq * N_HEADS)
    t, h = rows // N_HEADS, rows % N_HEADS
    slope_rows = np.repeat(slopes[h][:, None], LANES, axis=1).astype(np.float32)
    n_cols = SAMPLE_NEAR * N_HEADS + LANES
    cols = np.arange(n_cols)
    key = np.where(cols < SAMPLE_NEAR * N_HEADS, w_buf - SAMPLE_NEAR + cols // N_HEADS,
                   w_buf + (cols - SAMPLE_NEAR * N_HEADS) // N_HEADS)
    key_h = cols % N_HEADS
    real = key < w_buf + dec_seq
    dist = w_buf + t[:, None] - key[None, :]
    cnt = np.zeros(dist.shape, np.float64)
    for d in DILATIONS:
        cnt += (dist >= 0) & (dist % d == 0) & (dist <= d * ATTN_BLK)
    ok = (cnt > 0) & (key_h[None, :] == h[:, None]) & real[None, :]
    near = np.where(ok, -slopes[h][:, None] * dist + np.log(np.maximum(cnt, 1.0)), -np.inf).astype(np.float32)
    return slope_rows, near


def _attn_sample_kernel(q_ref, kn_ref, vn_ref, kf_ref, vf_ref, kc_ref, vc_ref, slope_ref, near_ref, o_ref,
                        *, w_buf):
    q = q_ref[...] * ATTN_SCALE

    d16 = DILATIONS[2]
    s_f = jnp.sum(kf_ref[0, 0, :, 0] * q[None], axis=-1, keepdims=True)
    dist_f = (w_buf - d16 * lax.broadcasted_iota(I32, s_f.shape, 0)).astype(F32)
    s_f = s_f - slope_ref[:, 0:1][None] * dist_f
    m_f = jnp.max(s_f, axis=0)
    e_f = jnp.exp(s_f - m_f[None])
    l_f = jnp.sum(e_f, axis=0)
    acc_f = jnp.sum(e_f * vf_ref[0, 0, :, 0], axis=0)

    n_near = SAMPLE_NEAR * N_HEADS
    pad = jnp.zeros((LANES - q.shape[0], HEAD_DIM), F32)
    kn = jnp.concatenate([kc_ref[0, 0].reshape(n_near, HEAD_DIM), kn_ref[...], pad], axis=0).astype(BF16)
    vn = jnp.concatenate([vc_ref[0, 0].reshape(n_near, HEAD_DIM), vn_ref[...], pad], axis=0).astype(BF16)
    s_n = lax.dot_general(q.astype(BF16), kn, (((1,), (1,)), ((), ())), preferred_element_type=F32)
    s_n = s_n + near_ref[...]
    m_n = jnp.max(s_n, axis=-1, keepdims=True)
    e_n = jnp.exp(s_n - m_n)
    l_n = jnp.sum(e_n, axis=-1, keepdims=True)
    acc_n = jnp.dot(e_n.astype(BF16), vn, preferred_element_type=F32)

    m = jnp.maximum(m_f, m_n)
    w_f = jnp.exp(m_f - m)
    w_n = jnp.exp(m_n - m)
    o_ref[...] = (w_f * acc_f + w_n * acc_n) / (w_f * l_f + w_n * l_n)


def _attn_sample(q64, k64, v64, cache_k6, cache_v6, tables, layer, dec_batch, dec_seq, w_buf):
    n_rows = dec_seq * N_HEADS
    group = DILATIONS[2]
    n_far = (w_buf - SAMPLE_NEAR) // group
    n_near = SAMPLE_NEAR // group
    assert n_far % n_near == 0 and dec_seq * 2 == group and n_rows <= LANES
    row_spec = pl.BlockSpec((n_rows, HEAD_DIM), lambda b: (b, 0))
    far_spec = pl.BlockSpec((1, 1, n_far, 1, n_rows, HEAD_DIM), lambda b: (layer, b, 0, 0, 0, 0))
    near_spec = pl.BlockSpec((1, 1, n_near, 2, n_rows, HEAD_DIM), lambda b: (layer, b, n_far // n_near, 0, 0, 0))
    slope_rows, near = tables
    return pl.pallas_call(
        functools.partial(_attn_sample_kernel, w_buf=w_buf),
        grid=(dec_batch,),
        in_specs=[row_spec, row_spec, row_spec, far_spec, far_spec, near_spec, near_spec,
                  pl.BlockSpec(slope_rows.shape, lambda b: (0, 0)),
                  pl.BlockSpec(near.shape, lambda b: (0, 0))],
        out_specs=row_spec,
        out_shape=jax.ShapeDtypeStruct((dec_batch * n_rows, HEAD_DIM), F32),
        compiler_params=_params("parallel"),
        name="attn_sample",
    )(q64, k64, v64, cache_k6, cache_v6, cache_k6, cache_v6, slope_rows, near)


def _layer_norm(z, g, b):
    mu = jnp.mean(z, axis=-1, keepdims=True)
    zc = z - mu
    var = jnp.mean(zc * zc, axis=-1, keepdims=True)
    return zc * lax.rsqrt(var + LN_EPS) * g + b


def _mix_route_kernel(x_ref, yl_ref, oa_ref, wout_ref, vec_ref, wr_ref, br_ref, tri_ref,
                      x1_ref, idx_ref, gate_ref, rank_ref, cnt_ref, carry_s):
    i = pl.program_id(0)

    @pl.when(i == 0)
    def _():
        carry_s[...] = jnp.zeros_like(carry_s)

    vec = vec_ref[0]
    oa = oa_ref[...]
    oa_n = (oa * lax.rsqrt(jnp.mean(oa * oa, axis=-1, keepdims=True) + RMS_EPS) * vec[0:1, :ATTN_WIDTH])
    mixed = (jnp.dot(yl_ref[...], wout_ref[0, :LRU_WIDTH], preferred_element_type=F32)
             + jnp.dot(oa_n.astype(BF16), wout_ref[0, LRU_WIDTH:], preferred_element_type=F32))
    x1 = _layer_norm(DEEPNORM_ALPHA * x_ref[...] + mixed, vec[1:2, :], vec[2:3, :])
    x1_ref[...] = x1

    logits = lax.dot_general(wr_ref[0], x1.astype(BF16), (((1,), (1,)), ((), ())),
                             preferred_element_type=F32) + br_ref[0]
    tile = logits.shape[1]
    e_iota = lax.broadcasted_iota(I32, logits.shape, 0).astype(F32)
    work = logits
    vals, idxs, sels = [], [], []
    for _ in range(TOP_K):
        mx = jnp.max(work, axis=0, keepdims=True)
        idx = jnp.min(jnp.where(work == mx, e_iota, float(N_EXPERTS)), axis=0, keepdims=True)
        sel = e_iota == idx
        vals.append(mx)
        idxs.append(idx)
        sels.append(sel)
        work = jnp.where(sel, -jnp.inf, work)
    exps = [jnp.exp(v - vals[0]) for v in vals]
    den = exps[0] + exps[1] + exps[2] + exps[3]
    gates = [e / den for e in exps]

    onehot = jnp.zeros(logits.shape, F32)
    for sel in sels:
        onehot = onehot + sel.astype(F32)
    prefix = jnp.dot(onehot.astype(BF16), tri_ref[...], preferred_element_type=F32) + carry_s[:, 0:1]
    ranks = [jnp.sum(jnp.where(sel, prefix, 0.0), axis=0, keepdims=True) for sel in sels]
    carry = carry_s[:, 0:1] + jnp.sum(onehot, axis=1, keepdims=True)
    carry_s[...] = jnp.broadcast_to(carry, carry_s.shape)
    cnt_ref[...] = jnp.broadcast_to(carry, cnt_ref.shape)

    row = lax.broadcasted_iota(I32, (SUBLANES, tile), 0)

    def stack(parts):
        out = jnp.zeros((SUBLANES, tile), F32)
        for k_, part in enumerate(parts):
            out = jnp.where(row == k_, part, out)
        return out

    idx_ref[...] = stack(idxs).astype(I32)
    gate_ref[...] = stack(gates)
    rank_ref[...] = stack(ranks).astype(I32)


def _mix_route(x, y_lru, o_attn, w_out_bf, vec1, wr_t, b_router, tri, layer):
    n = x.shape[0]
    t = TOKEN_TILE
    row = lambda w: pl.BlockSpec((t, w), lambda i: (i, 0))
    col = pl.BlockSpec((SUBLANES, t), lambda i: (0, i))
    return pl.pallas_call(
        _mix_route_kernel,
        grid=(n // t,),
        in_specs=[row(D_MODEL), row(LRU_WIDTH), row(ATTN_WIDTH),
                  pl.BlockSpec((1, D_MODEL, D_MODEL), lambda i: (layer, 0, 0)),
                  pl.BlockSpec((1, SUBLANES, D_MODEL), lambda i: (layer, 0, 0)),
                  pl.BlockSpec((1, N_EXPERTS, D_MODEL), lambda i: (layer, 0, 0)),
                  pl.BlockSpec((1, N_EXPERTS, 1), lambda i: (layer, 0, 0)),
                  pl.BlockSpec((t, t), lambda i: (0, 0))],
        out_specs=[row(D_MODEL), col, col, col,
                   pl.BlockSpec((N_EXPERTS, LANES), lambda i: (0, 0))],
        out_shape=[jax.ShapeDtypeStruct((n, D_MODEL), F32),
                   jax.ShapeDtypeStruct((SUBLANES, n), I32),
                   jax.ShapeDtypeStruct((SUBLANES, n), F32),
                   jax.ShapeDtypeStruct((SUBLANES, n), I32),
                   jax.ShapeDtypeStruct((N_EXPERTS, LANES), F32)],
        scratch_shapes=[pltpu.VMEM((N_EXPERTS, LANES), F32)],
        compiler_params=_params("arbitrary"),
        name="mix_route",
    )(x, y_lru, o_attn, w_out_bf, vec1, wr_t, b_router, tri)


def _fetch_indices(dest_hbm, idx_s, idx_sem, n_idx):
    i = pl.program_id(0)
    slot = i % 2

    def idx_copy(step, sl):
        return pltpu.make_async_copy(dest_hbm.at[pl.ds(pl.multiple_of(step * n_idx, n_idx), n_idx)],
                                     idx_s.at[sl], idx_sem.at[sl])

    @pl.when(i == 0)
    def _():
        idx_copy(0, 0).start()

    idx_copy(i, slot).wait()

    @pl.when(i + 1 < pl.num_programs(0))
    def _():
        idx_copy(i + 1, 1 - slot).start()

    return slot


def _dispatch_kernel(tok_hbm, x_hbm, xs_ref, idx_s, idx_sem, sem):
    slot = _fetch_indices(tok_hbm, idx_s, idx_sem, DISPATCH_TILE)

    def row_copy(r):
        return pltpu.make_async_copy(x_hbm.at[pl.ds(idx_s[slot, r], 1), :], xs_ref.at[pl.ds(r, 1), :], sem)

    def issue(r, carry):
        row_copy(r).start()
        return carry

    lax.fori_loop(0, DISPATCH_TILE, issue, 0, unroll=DMA_UNROLL)

    def drain(r, carry):
        row_copy(r).wait()
        return carry

    lax.fori_loop(0, DISPATCH_TILE, drain, 0, unroll=DMA_UNROLL)


def _dispatch(row_token, x1):
    n_rows = row_token.shape[0]
    return pl.pallas_call(
        _dispatch_kernel,
        grid=(n_rows // DISPATCH_TILE,),
        in_specs=[pl.BlockSpec(memory_space=pl.ANY), pl.BlockSpec(memory_space=pl.ANY)],
        out_specs=pl.BlockSpec((DISPATCH_TILE, D_MODEL), lambda i: (i, 0)),
        out_shape=jax.ShapeDtypeStruct((n_rows, D_MODEL), F32),
        scratch_shapes=[pltpu.SMEM((2, DISPATCH_TILE), I32),
                        pltpu.SemaphoreType.DMA((2,)), pltpu.SemaphoreType.DMA(())],
        compiler_params=_params("arbitrary"),
        name="dispatch",
    )(row_token, x1)


def _expert_kernel(te_ref, tf_ref, tv_ref, xs_ref, wu_ref, bu_ref, wd_ref, bd_ref, ys_ref, wub_s, wdb_s):
    del te_ref
    i = pl.program_id(0)

    @pl.when(tf_ref[i] == 1)
    def _():
        wub_s[...] = wu_ref[0, 0].astype(BF16)
        wdb_s[...] = wd_ref[0, 0].astype(BF16)

    @pl.when(tv_ref[i] == 0)
    def _():
        ys_ref[...] = jnp.zeros_like(ys_ref)

    @pl.when(tv_ref[i] == 1)
    def _():
        xb = xs_ref[...].astype(BF16)
        h = jnp.dot(xb, wub_s[...], preferred_element_type=F32) + bu_ref[0, 0]
        glu = jnp.minimum(h[:, :D_FF], SWIGLU_LIMIT)
        lin = jnp.clip(h[:, D_FF:], -SWIGLU_LIMIT, SWIGLU_LIMIT)
        act = glu * jax.nn.sigmoid(SWIGLU_ALPHA * glu) * (lin + 1.0)
        ys_ref[...] = jnp.dot(act.astype(BF16), wdb_s[...], preferred_element_type=F32) + bd_ref[0, 0]


def _experts(tile_expert, tile_first, tile_valid, xs, w_up, b_up4, w_down, b_down4, layer):
    n_rows = xs.shape[0]
    n_tiles = n_rows // EXPERT_TILE
    grid_spec = pltpu.PrefetchScalarGridSpec(
        num_scalar_prefetch=3,
        grid=(n_tiles,),
        in_specs=[pl.BlockSpec((EXPERT_TILE, D_MODEL), lambda i, te, tf, tv: (i, 0)),
                  pl.BlockSpec((1, 1, D_MODEL, 2 * D_FF), lambda i, te, tf, tv: (layer, te[i], 0, 0)),
                  pl.BlockSpec((1, 1, 1, 2 * D_FF), lambda i, te, tf, tv: (layer, te[i], 0, 0)),
                  pl.BlockSpec((1, 1, D_FF, D_MODEL), lambda i, te, tf, tv: (layer, te[i], 0, 0)),
                  pl.BlockSpec((1, 1, 1, D_MODEL), lambda i, te, tf, tv: (layer, te[i], 0, 0))],
        out_specs=pl.BlockSpec((EXPERT_TILE, D_MODEL), lambda i, te, tf, tv: (i, 0)),
        scratch_shapes=[pltpu.VMEM((D_MODEL, 2 * D_FF), BF16), pltpu.VMEM((D_FF, D_MODEL), BF16)],
    )
    return pl.pallas_call(
        _expert_kernel,
        grid_spec=grid_spec,
        out_shape=jax.ShapeDtypeStruct((n_rows, D_MODEL), F32),
        compiler_params=_params("arbitrary"),
        name="experts",
    )(tile_expert, tile_first, tile_valid, xs, w_up, b_up4, w_down, b_down4)


def _combine_kernel(dest_hbm, ys_hbm, x1_ref, gate_ref, pp_ref, ps_ref, vec_ref, wg_ref, wp_ref, o_ref,
                    idx_s, buf_s, idx_sem, sem, *, n_prompt_tiles):
    slot = _fetch_indices(dest_hbm, idx_s, idx_sem, TOP_K * COMBINE_TILE)

    def row_copy(t, k):
        return pltpu.make_async_copy(ys_hbm.at[pl.ds(idx_s[slot, t * TOP_K + k], 1), :],
                                     buf_s.at[k, pl.ds(t, 1), :], sem)

    def issue(t, carry):
        for k in range(TOP_K):
            row_copy(t, k).start()
        return carry

    lax.fori_loop(0, COMBINE_TILE, issue, 0, unroll=DMA_UNROLL // TOP_K)

    def drain(t, carry):
        for k in range(TOP_K):
            row_copy(t, k).wait()
        return carry

    lax.fori_loop(0, COMBINE_TILE, drain, 0, unroll=DMA_UNROLL // TOP_K)

    vec = vec_ref[0]
    g = gate_ref[...]
    ffn = g[:, 0:1] * buf_s[0]
    for k in range(1, TOP_K):
        ffn = ffn + g[:, k:k + 1] * buf_s[k]
    x2 = _layer_norm(DEEPNORM_ALPHA * x1_ref[...] + ffn, vec[3:4, :], vec[4:5, :])
    pg = jax.nn.sigmoid(jnp.dot(x2.astype(BF16), wg_ref[0], preferred_element_type=F32))
    ple = jnp.where(pl.program_id(0) < n_prompt_tiles, pp_ref[0], ps_ref[0])
    pp = jnp.dot(ple.astype(BF16), wp_ref[0], preferred_element_type=F32)
    o_ref[...] = x2 + pg * pp


def _combine(dest_flat, ys, x1, gates_col, ple_prompt, ple_sample, vec1, wg_bf, wp_bf, layer):
    n = x1.shape[0]
    t = COMBINE_TILE
    n_prompt_tiles = ple_prompt.shape[1] // t
    assert ple_prompt.shape[1] % t == 0 and ple_sample.shape[1] % t == 0
    return pl.pallas_call(
        functools.partial(_combine_kernel, n_prompt_tiles=n_prompt_tiles),
        grid=(n // t,),
        in_specs=[pl.BlockSpec(memory_space=pl.ANY),
                  pl.BlockSpec(memory_space=pl.ANY),
                  pl.BlockSpec((t, D_MODEL), lambda i: (i, 0)),
                  pl.BlockSpec((t, TOP_K), lambda i: (i, 0)),
                  pl.BlockSpec((1, t, PLE_DIM), lambda i: (layer, jnp.minimum(i, n_prompt_tiles - 1), 0)),
                  pl.BlockSpec((1, t, PLE_DIM), lambda i: (layer, jnp.maximum(i - n_prompt_tiles, 0), 0)),
                  pl.BlockSpec((1, SUBLANES, D_MODEL), lambda i: (layer, 0, 0)),
                  pl.BlockSpec((1, D_MODEL, D_MODEL), lambda i: (layer, 0, 0)),
                  pl.BlockSpec((1, PLE_DIM, D_MODEL), lambda i: (layer, 0, 0))],
        out_specs=pl.BlockSpec((t, D_MODEL), lambda i: (i, 0)),
        out_shape=jax.ShapeDtypeStruct((n, D_MODEL), F32),
        scratch_shapes=[pltpu.SMEM((2, TOP_K * t), I32),
                        pltpu.VMEM((TOP_K, t, D_MODEL), F32),
                        pltpu.SemaphoreType.DMA((2,)), pltpu.SemaphoreType.DMA(())],
        compiler_params=_params("arbitrary"),
        name="combine",
    )(dest_flat, ys, x1, gates_col, ple_prompt, ple_sample, vec1, wg_bf, wp_bf)


def _block_diag_pairs(w_a, w_x):
    depth = w_a.shape[0]
    eye = jnp.eye(4, dtype=w_a.dtype)

    def halves(w):
        w = w.reshape(depth, 2, 4, LRU_BLOCK_DIM, LRU_BLOCK_DIM)
        return jnp.einsum('lhnij,nm->lhnimj', w, eye).reshape(depth, 2, 256, 256)

    return jnp.concatenate([halves(w_a), halves(w_x)], axis=-1).astype(BF16)


def kernel(x_prompt, x_sample, cache_k, cache_v, state_conv, state_h, p_prompt, p_sample, w_in, conv_w, conv_b,
           w_gate_a, b_gate_a, w_gate_x, b_gate_x, lru_lambda, g_lru_norm, g_attn_norm, w_out, ln1_g, ln1_b,
           w_router, b_router, w_up, b_up, w_down, b_down, ln2_g, ln2_b, w_ple_proj, w_ple_gate):
    depth = w_in.shape[0]
    batch, seq, _ = x_prompt.shape
    dec_batch, dec_seq, _ = x_sample.shape
    w_buf = cache_k.shape[2]
    n_prompt, n_sample = batch * seq, dec_batch * dec_seq
    n = n_prompt + n_sample
    assert dec_seq == SUBLANES and seq % LRU_CHUNK == 0 and n % TOKEN_TILE == 0 and n_prompt % TOKEN_TILE == 0

    w_in_bf = w_in.astype(BF16)
    w_out_bf = w_out.astype(BF16)
    wg_bf = w_ple_gate.astype(BF16)
    wp_bf = w_ple_proj.astype(BF16)
    wr_t = jnp.swapaxes(w_router, 1, 2).astype(BF16)
    b_router3 = b_router[:, :, None]
    w2 = _block_diag_pairs(w_gate_a, w_gate_x)
    zeros512 = jnp.zeros((depth, 7, LRU_WIDTH), F32)
    lru_p = jnp.concatenate([conv_w, conv_b[:, None], b_gate_a[:, None], b_gate_x[:, None], lru_lambda[:, None],
                             g_lru_norm[:, None], zeros512], axis=1)
    pad_attn = jnp.zeros((depth, D_MODEL - ATTN_WIDTH), F32)
    vec1 = jnp.stack([jnp.concatenate([g_attn_norm, pad_attn], axis=1), ln1_g, ln1_b, ln2_g, ln2_b,
                      jnp.zeros_like(ln1_g), jnp.zeros_like(ln1_g), jnp.zeros_like(ln1_g)], axis=1)
    b_up4 = b_up[:, :, None, :]
    b_down4 = b_down[:, :, None, :]
    tri = jnp.asarray(np.triu(np.ones((TOKEN_TILE, TOKEN_TILE), np.float32), k=1), BF16)
    bias_np, bias16_np = _prompt_bias_tables()
    bias_p, bias16_p = jnp.asarray(bias_np), jnp.asarray(bias16_np)
    sample_tables = tuple(jnp.asarray(t) for t in _sample_tables(w_buf, dec_seq))

    group = DILATIONS[2]
    cache_shape6 = (depth, dec_batch, w_buf // group, 2, (group // 2) * N_HEADS, HEAD_DIM)
    cache_k6 = cache_k.reshape(cache_shape6)
    cache_v6 = cache_v.reshape(cache_shape6)
    cs_rows = jnp.pad(state_conv, ((0, 0), (0, 0), (SUBLANES - (CONV_WIDTH - 1), 0), (0, 0))
                      ).reshape(depth, n_sample, LRU_WIDTH)
    h0_rows = jnp.repeat(state_h, dec_seq, axis=1)
    cs_zero = jnp.zeros((batch, SUBLANES, LRU_WIDTH), F32)
    ple_prompt = p_prompt.reshape(depth, n_prompt, PLE_DIM)
    ple_sample = p_sample.reshape(depth, n_sample, PLE_DIM)

    n_tiles = (n * TOP_K) // EXPERT_TILE + N_EXPERTS
    n_rows = n_tiles * EXPERT_TILE
    tile_start = jnp.arange(n_tiles, dtype=I32) * EXPERT_TILE
    assert n_rows % DISPATCH_TILE == 0
    expert_ids = jnp.arange(N_EXPERTS, dtype=I32)
    token_ids = jnp.arange(n * TOP_K, dtype=I32) // TOP_K

    x = jnp.concatenate([x_prompt.reshape(n_prompt, D_MODEL), x_sample.reshape(n_sample, D_MODEL)], axis=0)
    kp, vp, cp, hp, ks, vs, cs, hs = [], [], [], [], [], [], [], []
    for layer in range(depth):
        u, gate, q, k, v = _inproj(x, w_in_bf, layer)

        y_lru_p, h_last = _lru_prompt(u, gate, cs_zero, cs_zero, lru_p, w2, layer, batch, seq)
        y_lru_s, h_all = _lru_sample(u, gate, cs_rows[layer], h0_rows[layer], lru_p, w2,
                                     layer, n_prompt, n_sample)
        y_lru = jnp.concatenate([y_lru_p, y_lru_s], axis=0)

        o_attn_p = _attn_prompt(q, k, v, bias_p, bias16_p, batch, seq)
        to_rows64 = lambda a: a[n_prompt:].reshape(n_sample * N_HEADS, HEAD_DIM)
        o_attn_s = _attn_sample(to_rows64(q), to_rows64(k), to_rows64(v), cache_k6, cache_v6, sample_tables,
                                layer, dec_batch, dec_seq, w_buf)
        o_attn = jnp.concatenate([o_attn_p, o_attn_s.reshape(n_sample, ATTN_WIDTH)], axis=0)

        x1, idx_t, gate_t, rank_t, cnt = _mix_route(x, y_lru, o_attn, w_out_bf, vec1, wr_t, b_router3, tri, layer)

        counts = cnt[:, 0].astype(I32)
        padded = (counts + EXPERT_TILE - 1) // EXPERT_TILE * EXPERT_TILE
        ends = jnp.cumsum(padded)
        starts = ends - padded
        start_of = jnp.sum(jnp.where(idx_t[:TOP_K, :, None] == expert_ids, starts, 0), axis=-1)
        dest_flat = (start_of + rank_t[:TOP_K]).T.reshape(-1)
        tile_expert = jnp.minimum(jnp.sum((tile_start[:, None] >= ends[None, :]).astype(I32), axis=1),
                                  N_EXPERTS - 1)
        tile_valid = (tile_start < ends[-1]).astype(I32)
        tile_expert = jnp.where(tile_valid == 1, tile_expert,
                                jnp.max(jnp.where(tile_valid == 1, tile_expert, 0)))
        changed = jnp.concatenate([jnp.ones((1,), I32), (tile_expert[1:] != tile_expert[:-1]).astype(I32)])
        tile_first = changed * tile_valid

        row_token = jnp.zeros((n_rows,), I32).at[dest_flat].set(token_ids)
        xs = _dispatch(row_token, x1)
        ys = _experts(tile_expert, tile_first, tile_valid, xs, w_up, b_up4, w_down, b_down4, layer)
        x = _combine(dest_flat, ys, x1, gate_t[:TOP_K].T, ple_prompt, ple_sample, vec1, wg_bf, wp_bf, layer)

        k_p = k[:n_prompt].reshape(batch, seq, N_HEADS, HEAD_DIM)
        v_p = v[:n_prompt].reshape(batch, seq, N_HEADS, HEAD_DIM)
        w_prompt = min(DILATIONS[2] * ATTN_BLK, seq)
        kp.append(k_p[:, -w_prompt:])
        vp.append(v_p[:, -w_prompt:])
        cp.append(u[:n_prompt].reshape(batch, seq, LRU_WIDTH)[:, seq - (CONV_WIDTH - 1):])
        hp.append(h_last[:, 0])
        ks.append(k[n_prompt:].reshape(dec_batch, dec_seq, N_HEADS, HEAD_DIM))
        vs.append(v[n_prompt:].reshape(dec_batch, dec_seq, N_HEADS, HEAD_DIM))
        cs.append(u[n_prompt:].reshape(dec_batch, dec_seq, LRU_WIDTH)[:, dec_seq - (CONV_WIDTH - 1):])
        hs.append(h_all.reshape(dec_batch, dec_seq, LRU_WIDTH)[:, dec_seq - 1])

    y_prompt = x[:n_prompt].reshape(batch, seq, D_MODEL)
    y_sample = x[n_prompt:].reshape(dec_batch, dec_seq, D_MODEL)
    return (y_prompt, y_sample, jnp.stack(kp), jnp.stack(vp), jnp.stack(cp), jnp.stack(hp),
            jnp.stack(ks), jnp.stack(vs), jnp.stack(cs), jnp.stack(hs))
```

```python
import functools

import numpy as np
import jax
import jax.numpy as jnp
from jax import lax
from jax.experimental import pallas as pl
from jax.experimental.pallas import tpu as pltpu

F32 = jnp.float32
BF16 = jnp.bfloat16
I32 = jnp.int32

D_MODEL = 1024
LRU_WIDTH = 512
LRU_BLOCK_DIM = 64
CONV_WIDTH = 4
LRU_C = 8.0
N_HEADS = 8
HEAD_DIM = 64
ATTN_WIDTH = N_HEADS * HEAD_DIM
ATTN_SCALE = HEAD_DIM ** -0.5
ATTN_BLK = 128
DILATIONS = (1, 4, 16)
N_EXPERTS = 32
TOP_K = 4
D_FF = 1024
SWIGLU_ALPHA = 1.702
SWIGLU_LIMIT = 7.0
PLE_DIM = 256
MODEL_DEPTH = 4
DEEPNORM_ALPHA = (2.0 * MODEL_DEPTH) ** 0.25
LN_EPS = 1e-5
RMS_EPS = 1e-6

SUBLANES = 8
LANES = 128
VMEM_LIMIT_BYTES = 56 * 1024 * 1024

TOKEN_TILE = 512
LRU_CHUNK = 256
EXPERT_TILE = 512
COMBINE_TILE = 512
DISPATCH_TILE = 1024
SAMPLE_NEAR = DILATIONS[1] * ATTN_BLK
ATTN_UNROLL = 4
DMA_UNROLL = 8


def _params(*sem):
    return pltpu.CompilerParams(dimension_semantics=sem, vmem_limit_bytes=VMEM_LIMIT_BYTES)


def _alibi_slopes():
    return [2.0 ** (-8.0 * (h + 1) / N_HEADS) for h in range(N_HEADS)]


def _inproj_kernel(x_ref, w_ref, u_ref, g_ref, q_ref, k_ref, v_ref):
    xb = x_ref[...].astype(BF16)
    for i, o_ref in enumerate((u_ref, g_ref, q_ref, k_ref, v_ref)):
        o_ref[...] = jnp.dot(xb, w_ref[0, :, i * 512:(i + 1) * 512], preferred_element_type=F32)


def _inproj(x, w_in_bf, layer):
    n = x.shape[0]
    out = jax.ShapeDtypeStruct((n, 512), F32)
    return pl.pallas_call(
        _inproj_kernel,
        grid=(n // TOKEN_TILE,),
        in_specs=[pl.BlockSpec((TOKEN_TILE, D_MODEL), lambda i: (i, 0)),
                  pl.BlockSpec((1, D_MODEL, 2560), lambda i: (layer, 0, 0))],
        out_specs=[pl.BlockSpec((TOKEN_TILE, 512), lambda i: (i, 0))] * 5,
        out_shape=[out] * 5,
        compiler_params=_params("parallel"),
        name="inproj",
    )(x, w_in_bf)


def _lru_coeffs(xc, p, w2_ref):
    xb = xc.astype(BF16)
    g0 = jnp.dot(xb[:, :256], w2_ref[0, 0], preferred_element_type=F32)
    g1 = jnp.dot(xb[:, 256:], w2_ref[0, 1], preferred_element_type=F32)
    ga = jnp.concatenate([g0[:, :256], g1[:, :256]], axis=1) + p[5:6, :]
    gx = jnp.concatenate([g0[:, 256:], g1[:, 256:]], axis=1) + p[6:7, :]
    r = jax.nn.sigmoid(ga)
    i = jax.nn.sigmoid(gx)
    z = -p[7:8, :]
    softplus = jnp.maximum(z, 0.0) + jnp.log1p(jnp.exp(-jnp.abs(z)))
    log_a = (-LRU_C) * r * softplus
    a = jnp.exp(log_a)
    b = jnp.sqrt(-jnp.tanh(log_a) * (a * a + 1.0)) * i * xc
    return a, b


def _group_scan(a, b):
    row = lax.broadcasted_iota(I32, a.shape, 0) & (SUBLANES - 1)
    for s in (1, 2, 4):
        a_prev = pltpu.roll(a, s, 0)
        b_prev = pltpu.roll(b, s, 0)
        m = row >= s
        b = jnp.where(m, a * b_prev + b, b)
        a = jnp.where(m, a * a_prev, a)
    return a, b


def _lru_finish(h, gate, p):
    y = h * jax.nn.gelu(gate)
    var = jnp.mean(y * y, axis=-1, keepdims=True)
    return (y * lax.rsqrt(var + RMS_EPS) * p[8:9, :]).astype(BF16)


def _lru_prompt_kernel(u_ref, gate_ref, cs_ref, h0_ref, p_ref, w2_ref, y_ref, hlast_ref, tail_s, hc_s):
    c = pl.program_id(1)

    @pl.when(c == 0)
    def _():
        tail_s[...] = cs_ref[0]
        hc_s[...] = h0_ref[0]

    p = p_ref[0]
    u = u_ref[...]
    tail = tail_s[...]
    row8 = lax.broadcasted_iota(I32, (SUBLANES, LRU_WIDTH), 0)
    xc = p[4:5, :] + p[3:4, :] * u
    for s in (1, 2, 3):
        sh = pltpu.roll(u, s, 0)
        first = jnp.where(row8 < s, pltpu.roll(tail, s, 0), sh[0:SUBLANES])
        xc = xc + p[3 - s:4 - s, :] * jnp.concatenate([first, sh[SUBLANES:]], axis=0)
    tail_s[...] = u[LRU_CHUNK - SUBLANES:]

    a, b = _lru_coeffs(xc, p, w2_ref)
    a, b = _group_scan(a, b)
    h = hc_s[0:1, :]
    hs = []
    for g in range(LRU_CHUNK // SUBLANES):
        sl = slice(g * SUBLANES, (g + 1) * SUBLANES)
        hg = a[sl] * h + b[sl]
        hs.append(hg)
        h = hg[SUBLANES - 1:SUBLANES, :]
    hc_s[...] = jnp.broadcast_to(h, (SUBLANES, LRU_WIDTH))
    hlast_ref[0] = jnp.broadcast_to(h, (SUBLANES, LRU_WIDTH))
    y_ref[...] = _lru_finish(jnp.concatenate(hs, axis=0), gate_ref[...], p)


def _lru_prompt(u, gate, cs8, h08, lru_p, w2, layer, batch, seq):
    n_chunks = seq // LRU_CHUNK
    n_total = batch * seq
    return pl.pallas_call(
        _lru_prompt_kernel,
        grid=(batch, n_chunks),
        in_specs=[pl.BlockSpec((LRU_CHUNK, LRU_WIDTH), lambda b, c: (b * n_chunks + c, 0)),
                  pl.BlockSpec((LRU_CHUNK, LRU_WIDTH), lambda b, c: (b * n_chunks + c, 0)),
                  pl.BlockSpec((1, SUBLANES, LRU_WIDTH), lambda b, c: (b, 0, 0)),
                  pl.BlockSpec((1, SUBLANES, LRU_WIDTH), lambda b, c: (b, 0, 0)),
                  pl.BlockSpec((1, 16, LRU_WIDTH), lambda b, c: (layer, 0, 0)),
                  pl.BlockSpec((1, 2, 256, 512), lambda b, c: (layer, 0, 0, 0))],
        out_specs=[pl.BlockSpec((LRU_CHUNK, LRU_WIDTH), lambda b, c: (b * n_chunks + c, 0)),
                   pl.BlockSpec((1, SUBLANES, LRU_WIDTH), lambda b, c: (b, 0, 0))],
        out_shape=[jax.ShapeDtypeStruct((n_total, LRU_WIDTH), BF16),
                   jax.ShapeDtypeStruct((batch, SUBLANES, LRU_WIDTH), F32)],
        scratch_shapes=[pltpu.VMEM((SUBLANES, LRU_WIDTH), F32), pltpu.VMEM((SUBLANES, LRU_WIDTH), F32)],
        compiler_params=_params("arbitrary", "arbitrary"),
        name="lru_prompt",
    )(u, gate, cs8, h08, lru_p, w2)


def _lru_sample_kernel(u_ref, gate_ref, cs_ref, h0_ref, p_ref, w2_ref, y_ref, h_ref):
    p = p_ref[0]
    u = u_ref[...]
    cs = cs_ref[...]
    rows = u.shape[0]
    row = lax.broadcasted_iota(I32, u.shape, 0) & (SUBLANES - 1)
    xc = p[4:5, :] + p[3:4, :] * u
    for s in (1, 2, 3):
        sh = pltpu.roll(u, s, 0)
        cs_sh = pltpu.roll(cs, rows - SUBLANES + s, 0)
        xc = xc + p[3 - s:4 - s, :] * jnp.where(row < s, cs_sh, sh)
    a, b = _lru_coeffs(xc, p, w2_ref)
    a, b = _group_scan(a, b)
    h = a * h0_ref[...] + b
    h_ref[...] = h
    y_ref[...] = _lru_finish(h, gate_ref[...], p)


def _lru_sample(u, gate, cs_rows, h0_rows, lru_p, w2, layer, n_prompt, n_sample):
    tile = min(256, n_sample)
    off = n_prompt // tile
    return pl.pallas_call(
        _lru_sample_kernel,
        grid=(n_sample // tile,),
        in_specs=[pl.BlockSpec((tile, LRU_WIDTH), lambda i: (off + i, 0)),
                  pl.BlockSpec((tile, LRU_WIDTH), lambda i: (off + i, 0)),
                  pl.BlockSpec((tile, LRU_WIDTH), lambda i: (i, 0)),
                  pl.BlockSpec((tile, LRU_WIDTH), lambda i: (i, 0)),
                  pl.BlockSpec((1, 16, LRU_WIDTH), lambda i: (layer, 0, 0)),
                  pl.BlockSpec((1, 2, 256, 512), lambda i: (layer, 0, 0, 0))],
        out_specs=[pl.BlockSpec((tile, LRU_WIDTH), lambda i: (i, 0)),
                   pl.BlockSpec((tile, LRU_WIDTH), lambda i: (i, 0))],
        out_shape=[jax.ShapeDtypeStruct((n_sample, LRU_WIDTH), BF16),
                   jax.ShapeDtypeStruct((n_sample, LRU_WIDTH), F32)],
        compiler_params=_params("parallel"),
        name="lru_sample",
    )(u, gate, cs_rows, h0_rows, lru_p, w2)


def _prompt_bias_tables():
    slopes = _alibi_slopes()
    q = np.arange(ATTN_BLK)[:, None]
    k = np.arange(2 * ATTN_BLK)[None, :]
    steps = q + ATTN_BLK - k
    valid = (steps >= 0) & (steps <= ATTN_BLK)
    steps_f = q - k
    valid_f = (k < ATTN_BLK) & (steps_f >= 0)
    table = np.full((N_HEADS, 2, 2, ATTN_BLK, 2 * ATTN_BLK), -np.inf, np.float32)
    table16 = np.full((N_HEADS, ATTN_BLK, ATTN_BLK), -np.inf, np.float32)
    for h in range(N_HEADS):
        for di, d in enumerate(DILATIONS[:2]):
            table[h, di, 0] = np.where(valid, -slopes[h] * d * steps, -np.inf)
            table[h, di, 1] = np.where(valid_f, -slopes[h] * d * steps_f, -np.inf)
        s16 = q - k[:, :ATTN_BLK]
        table16[h] = np.where(s16 >= 0, -slopes[h] * DILATIONS[2] * s16, -np.inf)
    return table, table16


def _attn_prompt_kernel(q_ref, k_ref, v_ref, bias_ref, bias16_ref, o_ref, acc_s, m_s, l_s, *, seq):
    lane = lax.broadcasted_iota(I32, (ATTN_BLK, LANES), 1)
    lo = lane < HEAD_DIM

    def rows(start, size, stride):
        return pl.ds(start, size) if stride == 1 else pl.ds(start, size, stride=stride)

    def tile(di, qstart, kstart, nk, stride, bias_of_head):
        qsl = rows(qstart, ATTN_BLK, stride)
        ksl = rows(kstart, nk, stride)
        qs = q_ref[qsl, :] * ATTN_SCALE
        kk = k_ref[ksl, :].astype(BF16)
        vv = v_ref[ksl, :].astype(BF16)
        parts = []
        for hh in (0, 1):
            qm = jnp.where(lo if hh == 0 else jnp.logical_not(lo), qs, 0.0).astype(BF16)
            s = lax.dot_general(qm, kk, (((1,), (1,)), ((), ())), preferred_element_type=F32)
            s = s + bias_of_head(hh)
            m = jnp.max(s, axis=-1, keepdims=True)
            e = jnp.exp(s - m)
            l = jnp.sum(e, axis=-1, keepdims=True)
            pv = jnp.dot(e.astype(BF16), vv, preferred_element_type=F32)
            parts.append((pv, m, l))
        (pv0, m0, l0), (pv1, m1, l1) = parts
        acc_s[di, qsl, :] = jnp.where(lo, pv0, pv1)
        m_s[di, qsl, :] = jnp.where(lo, m0, m1)
        l_s[di, qsl, :] = jnp.where(lo, l0, l1)

    d1, d4, d16 = DILATIONS
    span4 = d4 * ATTN_BLK

    def body1(it, carry):
        for u in range(ATTN_UNROLL):
            j = it * ATTN_UNROLL + u
            qstart = pl.multiple_of(ATTN_BLK * j, ATTN_BLK)
            if u == 0:
                first = 1 - jnp.minimum(j, 1)
                kstart = pl.multiple_of(ATTN_BLK * jnp.maximum(j - 1, 0), ATTN_BLK)
            else:
                first = 0
                kstart = pl.multiple_of(ATTN_BLK * (j - 1), ATTN_BLK)
            tile(0, qstart, kstart, 2 * ATTN_BLK, d1, lambda hh, first=first: bias_ref[hh, 0, first])
        return carry

    lax.fori_loop(0, seq // (ATTN_BLK * ATTN_UNROLL), body1, 0)

    def body4(r, carry):
        for j in range(seq // span4):
            tile(1, r + span4 * j, r + span4 * max(j - 1, 0), 2 * ATTN_BLK, d4,
                 lambda hh, first=int(j == 0): bias_ref[hh, 1, first])
        return carry

    lax.fori_loop(0, d4, body4, 0)

    def body16(it, carry):
        for u in range(ATTN_UNROLL):
            r = it * ATTN_UNROLL + u
            tile(2, r, r, ATTN_BLK, d16, lambda hh: bias16_ref[hh])
        return carry

    lax.fori_loop(0, d16 // ATTN_UNROLL, body16, 0)

    chunk = 256

    def merge(ci, carry):
        sl = pl.ds(pl.multiple_of(ci * chunk, chunk), chunk)
        m0, m1, m2 = m_s[0, sl, :], m_s[1, sl, :], m_s[2, sl, :]
        mm = jnp.maximum(jnp.maximum(m0, m1), m2)
        w0, w1, w2 = jnp.exp(m0 - mm), jnp.exp(m1 - mm), jnp.exp(m2 - mm)
        den = w0 * l_s[0, sl, :] + w1 * l_s[1, sl, :] + w2 * l_s[2, sl, :]
        num = w0 * acc_s[0, sl, :] + w1 * acc_s[1, sl, :] + w2 * acc_s[2, sl, :]
        o_ref[sl, :] = num / den
        return carry

    lax.fori_loop(0, seq // chunk, merge, 0)


def _attn_prompt(q, k, v, bias, bias16, batch, seq):
    assert seq == DILATIONS[2] * ATTN_BLK, "dilation-16 pass assumes one tile per residue class"
    hp = N_HEADS // 2
    n_total = batch * seq
    qkv_spec = pl.BlockSpec((seq, LANES), lambda b, p: (b, p))
    return pl.pallas_call(
        functools.partial(_attn_prompt_kernel, seq=seq),
        grid=(batch, hp),
        in_specs=[qkv_spec, qkv_spec, qkv_spec,
                  pl.BlockSpec((2, 2, 2, ATTN_BLK, 2 * ATTN_BLK), lambda b, p: (p, 0, 0, 0, 0)),
                  pl.BlockSpec((2, ATTN_BLK, ATTN_BLK), lambda b, p: (p, 0, 0))],
        out_specs=pl.BlockSpec((seq, LANES), lambda b, p: (b, p)),
        out_shape=jax.ShapeDtypeStruct((n_total, ATTN_WIDTH), F32),
        scratch_shapes=[pltpu.VMEM((3, seq, LANES), F32)] * 3,
        compiler_params=_params("parallel", "parallel"),
        name="attn_prompt",
    )(q, k, v, bias, bias16)


def _sample_tables(w_buf, dec_seq):
    slopes = np.asarray(_alibi_slopes())
    rows = np.arange(dec_seq * N_HEADS)
    t, h = rows // N_HEADS, rows % N_HEADS
    slope_rows = np.repeat(slopes[h][:, None], LANES, axis=1).astype(np.float32)
    n_cols = SAMPLE_NEAR * N_HEADS + LANES
    cols = np.arange(n_cols)
    key = np.where(cols < SAMPLE_NEAR * N_HEADS, w_buf - SAMPLE_NEAR + cols // N_HEADS,
                   w_buf + (cols - SAMPLE_NEAR * N_HEADS) // N_HEADS)
    key_h = cols % N_HEADS
    real = key < w_buf + dec_seq
    dist = w_buf + t[:, None] - key[None, :]
    cnt = np.zeros(dist.shape, np.float64)
    for d in DILATIONS:
        cnt += (dist >= 0) & (dist % d == 0) & (dist <= d * ATTN_BLK)
    ok = (cnt > 0) & (key_h[None, :] == h[:, None]) & real[None, :]
    near = np.where(ok, -slopes[h][:, None] * dist + np.log(np.maximum(cnt, 1.0)), -np.inf).astype(np.float32)
    return slope_rows, near


def _attn_sample_kernel(q_ref, kn_ref, vn_ref, ck_ref, cv_ref, slope_ref, near_ref, o_ref, *, w_buf):
    q = q_ref[...] * ATTN_SCALE
    d16 = DILATIONS[2]
    n_far_keys = w_buf - SAMPLE_NEAR
    n_far = n_far_keys // d16

    def far(ref):
        x = ref[0, 0, pl.ds(0, n_far_keys)].reshape(n_far, d16, N_HEADS, HEAD_DIM)[:, :d16 // 2]
        return x.reshape(n_far, (d16 // 2) * N_HEADS, HEAD_DIM)

    def near(ref):
        return ref[0, 0, pl.ds(n_far_keys, SAMPLE_NEAR)].reshape(SAMPLE_NEAR * N_HEADS, HEAD_DIM)

    s_f = jnp.sum(far(ck_ref) * q[None], axis=-1, keepdims=True)
    dist_f = (w_buf - d16 * lax.broadcasted_iota(I32, s_f.shape, 0)).astype(F32)
    s_f = s_f - slope_ref[:, 0:1][None] * dist_f
    m_f = jnp.max(s_f, axis=0)
    e_f = jnp.exp(s_f - m_f[None])
    l_f = jnp.sum(e_f, axis=0)
    acc_f = jnp.sum(e_f * far(cv_ref), axis=0)

    pad = jnp.zeros((LANES - q.shape[0], HEAD_DIM), F32)
    kn = jnp.concatenate([near(ck_ref), kn_ref[...], pad], axis=0).astype(BF16)
    vn = jnp.concatenate([near(cv_ref), vn_ref[...], pad], axis=0).astype(BF16)
    s_n = lax.dot_general(q.astype(BF16), kn, (((1,), (1,)), ((), ())), preferred_element_type=F32)
    s_n = s_n + near_ref[...]
    m_n = jnp.max(s_n, axis=-1, keepdims=True)
    e_n = jnp.exp(s_n - m_n)
    l_n = jnp.sum(e_n, axis=-1, keepdims=True)
    acc_n = jnp.dot(e_n.astype(BF16), vn, preferred_element_type=F32)

    m = jnp.maximum(m_f, m_n)
    w_f = jnp.exp(m_f - m)
    w_n = jnp.exp(m_n - m)
    o_ref[...] = (w_f * acc_f + w_n * acc_n) / (w_f * l_f + w_n * l_n)


def _attn_sample(q64, k64, v64, cache_k, cache_v, tables, layer, dec_batch, dec_seq, w_buf):
    n_rows = dec_seq * N_HEADS
    assert dec_seq * 2 == DILATIONS[2] and n_rows <= LANES and (w_buf - SAMPLE_NEAR) % DILATIONS[2] == 0
    row_spec = pl.BlockSpec((n_rows, HEAD_DIM), lambda b: (b, 0))
    cache_spec = pl.BlockSpec((1, 1, w_buf, N_HEADS, HEAD_DIM), lambda b: (layer, b, 0, 0, 0))
    slope_rows, near = tables
    return pl.pallas_call(
        functools.partial(_attn_sample_kernel, w_buf=w_buf),
        grid=(dec_batch,),
        in_specs=[row_spec, row_spec, row_spec, cache_spec, cache_spec,
                  pl.BlockSpec(slope_rows.shape, lambda b: (0, 0)),
                  pl.BlockSpec(near.shape, lambda b: (0, 0))],
        out_specs=row_spec,
        out_shape=jax.ShapeDtypeStruct((dec_batch * n_rows, HEAD_DIM), F32),
        compiler_params=_params("parallel"),
        name="attn_sample",
    )(q64, k64, v64, cache_k, cache_v, slope_rows, near)


def _layer_norm(z, g, b):
    mu = jnp.mean(z, axis=-1, keepdims=True)
    zc = z - mu
    var = jnp.mean(zc * zc, axis=-1, keepdims=True)
    return zc * lax.rsqrt(var + LN_EPS) * g + b


def _mix_route_kernel(x_ref, ylp_ref, yls_ref, oap_ref, oas_ref, wout_ref, vec_ref, wr_ref, br_ref, tri_ref,
                      x1_ref, idx_ref, gate_ref, rank_ref, cnt_ref, carry_s, *, n_prompt_tiles):
    i = pl.program_id(0)

    @pl.when(i == 0)
    def _():
        carry_s[...] = jnp.zeros_like(carry_s)

    vec = vec_ref[0]
    is_prompt = i < n_prompt_tiles
    oa = jnp.where(is_prompt, oap_ref[...], oas_ref[...])
    yl = jnp.where(is_prompt, ylp_ref[...], yls_ref[...])
    oa_n = (oa * lax.rsqrt(jnp.mean(oa * oa, axis=-1, keepdims=True) + RMS_EPS) * vec[0:1, :ATTN_WIDTH])
    mixed = (jnp.dot(yl, wout_ref[0, :LRU_WIDTH], preferred_element_type=F32)
             + jnp.dot(oa_n.astype(BF16), wout_ref[0, LRU_WIDTH:], preferred_element_type=F32))
    x1 = _layer_norm(DEEPNORM_ALPHA * x_ref[...] + mixed, vec[1:2, :], vec[2:3, :])
    x1_ref[...] = x1

    logits = lax.dot_general(wr_ref[0], x1.astype(BF16), (((1,), (1,)), ((), ())),
                             preferred_element_type=F32) + br_ref[0]
    tile = logits.shape[1]
    e_iota = lax.broadcasted_iota(I32, logits.shape, 0).astype(F32)
    work = logits
    vals, idxs, sels = [], [], []
    for _ in range(TOP_K):
        mx = jnp.max(work, axis=0, keepdims=True)
        idx = jnp.min(jnp.where(work == mx, e_iota, float(N_EXPERTS)), axis=0, keepdims=True)
        sel = e_iota == idx
        vals.append(mx)
        idxs.append(idx)
        sels.append(sel)
        work = jnp.where(sel, -jnp.inf, work)
    exps = [jnp.exp(v - vals[0]) for v in vals]
    den = exps[0] + exps[1] + exps[2] + exps[3]
    gates = [e / den for e in exps]

    onehot = jnp.zeros(logits.shape, F32)
    for sel in sels:
        onehot = onehot + sel.astype(F32)
    prefix = jnp.dot(onehot.astype(BF16), tri_ref[...], preferred_element_type=F32) + carry_s[:, 0:1]
    ranks = [jnp.sum(jnp.where(sel, prefix, 0.0), axis=0, keepdims=True) for sel in sels]
    carry = carry_s[:, 0:1] + jnp.sum(onehot, axis=1, keepdims=True)
    carry_s[...] = jnp.broadcast_to(carry, carry_s.shape)
    cnt_ref[...] = jnp.broadcast_to(carry, cnt_ref.shape)

    row = lax.broadcasted_iota(I32, (SUBLANES, tile), 0)

    def stack(parts):
        out = jnp.zeros((SUBLANES, tile), F32)
        for k_, part in enumerate(parts):
            out = jnp.where(row == k_, part, out)
        return out

    idx_ref[...] = stack(idxs).astype(I32)
    gate_ref[...] = stack(gates)
    rank_ref[...] = stack(ranks).astype(I32)


def _mix_route(x, y_lru_p, y_lru_s, o_attn_p, o_attn_s, w_out_bf, vec1, wr_t, b_router, tri, layer):
    n = x.shape[0]
    t = TOKEN_TILE
    n_prompt_tiles = y_lru_p.shape[0] // t
    assert y_lru_p.shape[0] % t == 0 and y_lru_s.shape[0] % t == 0
    row = lambda w: pl.BlockSpec((t, w), lambda i: (i, 0))
    prompt_row = lambda w: pl.BlockSpec((t, w), lambda i: (jnp.minimum(i, n_prompt_tiles - 1), 0))
    sample_row = lambda w: pl.BlockSpec((t, w), lambda i: (jnp.maximum(i - n_prompt_tiles, 0), 0))
    col = pl.BlockSpec((SUBLANES, t), lambda i: (0, i))
    return pl.pallas_call(
        functools.partial(_mix_route_kernel, n_prompt_tiles=n_prompt_tiles),
        grid=(n // t,),
        in_specs=[row(D_MODEL), prompt_row(LRU_WIDTH), sample_row(LRU_WIDTH),
                  prompt_row(ATTN_WIDTH), sample_row(ATTN_WIDTH),
                  pl.BlockSpec((1, D_MODEL, D_MODEL), lambda i: (layer, 0, 0)),
                  pl.BlockSpec((1, SUBLANES, D_MODEL), lambda i: (layer, 0, 0)),
                  pl.BlockSpec((1, N_EXPERTS, D_MODEL), lambda i: (layer, 0, 0)),
                  pl.BlockSpec((1, N_EXPERTS, 1), lambda i: (layer, 0, 0)),
                  pl.BlockSpec((t, t), lambda i: (0, 0))],
        out_specs=[row(D_MODEL), col, col, col,
                   pl.BlockSpec((N_EXPERTS, LANES), lambda i: (0, 0))],
        out_shape=[jax.ShapeDtypeStruct((n, D_MODEL), F32),
                   jax.ShapeDtypeStruct((SUBLANES, n), I32),
                   jax.ShapeDtypeStruct((SUBLANES, n), F32),
                   jax.ShapeDtypeStruct((SUBLANES, n), I32),
                   jax.ShapeDtypeStruct((N_EXPERTS, LANES), F32)],
        scratch_shapes=[pltpu.VMEM((N_EXPERTS, LANES), F32)],
        compiler_params=_params("arbitrary"),
        name="mix_route",
    )(x, y_lru_p, y_lru_s, o_attn_p, o_attn_s, w_out_bf, vec1, wr_t, b_router, tri)


def _fetch_indices(dest_hbm, idx_s, idx_sem, n_idx):
    i = pl.program_id(0)
    slot = i % 2

    def idx_copy(step, sl):
        return pltpu.make_async_copy(dest_hbm.at[pl.ds(pl.multiple_of(step * n_idx, n_idx), n_idx)],
                                     idx_s.at[sl], idx_sem.at[sl])

    @pl.when(i == 0)
    def _():
        idx_copy(0, 0).start()

    idx_copy(i, slot).wait()

    @pl.when(i + 1 < pl.num_programs(0))
    def _():
        idx_copy(i + 1, 1 - slot).start()

    return slot


def _dispatch_kernel(tok_hbm, x_hbm, xs_ref, idx_s, idx_sem, sem):
    slot = _fetch_indices(tok_hbm, idx_s, idx_sem, DISPATCH_TILE)

    def row_copy(r):
        return pltpu.make_async_copy(x_hbm.at[pl.ds(idx_s[slot, r], 1), :], xs_ref.at[pl.ds(r, 1), :], sem)

    def issue(it, carry):
        for j in range(DMA_UNROLL):
            row_copy(it * DMA_UNROLL + j).start(priority=j % 2)
        return carry

    lax.fori_loop(0, DISPATCH_TILE // DMA_UNROLL, issue, 0)

    def drain(it, carry):
        for j in range(DMA_UNROLL):
            row_copy(it * DMA_UNROLL + j).wait()
        return carry

    lax.fori_loop(0, DISPATCH_TILE // DMA_UNROLL, drain, 0)


def _dispatch(row_token, x1):
    n_rows = row_token.shape[0]
    return pl.pallas_call(
        _dispatch_kernel,
        grid=(n_rows // DISPATCH_TILE,),
        in_specs=[pl.BlockSpec(memory_space=pl.ANY), pl.BlockSpec(memory_space=pl.ANY)],
        out_specs=pl.BlockSpec((DISPATCH_TILE, D_MODEL), lambda i: (i, 0)),
        out_shape=jax.ShapeDtypeStruct((n_rows, D_MODEL), F32),
        scratch_shapes=[pltpu.SMEM((2, DISPATCH_TILE), I32),
                        pltpu.SemaphoreType.DMA((2,)), pltpu.SemaphoreType.DMA(())],
        compiler_params=_params("arbitrary"),
        name="dispatch",
    )(row_token, x1)


def _expert_kernel(te_ref, tf_ref, tv_ref, xs_ref, wu_ref, bu_ref, wd_ref, bd_ref, ys_ref, wub_s, wdb_s):
    del te_ref
    i = pl.program_id(0)

    @pl.when(tf_ref[i] == 1)
    def _():
        wub_s[...] = wu_ref[0, 0].astype(BF16)
        wdb_s[...] = wd_ref[0, 0].astype(BF16)

    @pl.when(tv_ref[i] == 0)
    def _():
        ys_ref[...] = jnp.zeros_like(ys_ref)

    @pl.when(tv_ref[i] == 1)
    def _():
        xb = xs_ref[...].astype(BF16)
        h = jnp.dot(xb, wub_s[...], preferred_element_type=F32) + bu_ref[0, 0]
        glu = jnp.minimum(h[:, :D_FF], SWIGLU_LIMIT)
        lin = jnp.clip(h[:, D_FF:], -SWIGLU_LIMIT, SWIGLU_LIMIT)
        act = glu * jax.nn.sigmoid(SWIGLU_ALPHA * glu) * (lin + 1.0)
        ys_ref[...] = jnp.dot(act.astype(BF16), wdb_s[...], preferred_element_type=F32) + bd_ref[0, 0]


def _experts(tile_expert, tile_first, tile_valid, xs, w_up, b_up4, w_down, b_down4, layer):
    n_rows = xs.shape[0]
    n_tiles = n_rows // EXPERT_TILE
    grid_spec = pltpu.PrefetchScalarGridSpec(
        num_scalar_prefetch=3,
        grid=(n_tiles,),
        in_specs=[pl.BlockSpec((EXPERT_TILE, D_MODEL), lambda i, te, tf, tv: (i, 0)),
                  pl.BlockSpec((1, 1, D_MODEL, 2 * D_FF), lambda i, te, tf, tv: (layer, te[i], 0, 0)),
                  pl.BlockSpec((1, 1, 1, 2 * D_FF), lambda i, te, tf, tv: (layer, te[i], 0, 0)),
                  pl.BlockSpec((1, 1, D_FF, D_MODEL), lambda i, te, tf, tv: (layer, te[i], 0, 0)),
                  pl.BlockSpec((1, 1, 1, D_MODEL), lambda i, te, tf, tv: (layer, te[i], 0, 0))],
        out_specs=pl.BlockSpec((EXPERT_TILE, D_MODEL), lambda i, te, tf, tv: (i, 0)),
        scratch_shapes=[pltpu.VMEM((D_MODEL, 2 * D_FF), BF16), pltpu.VMEM((D_FF, D_MODEL), BF16)],
    )
    return pl.pallas_call(
        _expert_kernel,
        grid_spec=grid_spec,
        out_shape=jax.ShapeDtypeStruct((n_rows, D_MODEL), F32),
        compiler_params=_params("arbitrary"),
        name="experts",
    )(tile_expert, tile_first, tile_valid, xs, w_up, b_up4, w_down, b_down4)


def _combine_kernel(dest_hbm, ys_hbm, x1_ref, gate_ref, pp_ref, ps_ref, vec_ref, wg_ref, wp_ref, o_ref,
                    idx_s, buf_s, idx_sem, sem, *, n_prompt_tiles):
    slot = _fetch_indices(dest_hbm, idx_s, idx_sem, TOP_K * COMBINE_TILE)

    def row_copy(t, k):
        return pltpu.make_async_copy(ys_hbm.at[pl.ds(idx_s[slot, t * TOP_K + k], 1), :],
                                     buf_s.at[k, pl.ds(t, 1), :], sem)

    tokens_per_iter = DMA_UNROLL // TOP_K

    def issue(it, carry):
        for j in range(tokens_per_iter):
            for k in range(TOP_K):
                row_copy(it * tokens_per_iter + j, k).start(priority=k % 2)
        return carry

    lax.fori_loop(0, COMBINE_TILE // tokens_per_iter, issue, 0)

    def drain(it, carry):
        for j in range(tokens_per_iter):
            for k in range(TOP_K):
                row_copy(it * tokens_per_iter + j, k).wait()
        return carry

    lax.fori_loop(0, COMBINE_TILE // tokens_per_iter, drain, 0)

    vec = vec_ref[0]
    g = gate_ref[...]
    ffn = g[:, 0:1] * buf_s[0]
    for k in range(1, TOP_K):
        ffn = ffn + g[:, k:k + 1] * buf_s[k]
    x2 = _layer_norm(DEEPNORM_ALPHA * x1_ref[...] + ffn, vec[3:4, :], vec[4:5, :])
    pg = jax.nn.sigmoid(jnp.dot(x2.astype(BF16), wg_ref[0], preferred_element_type=F32))
    ple = jnp.where(pl.program_id(0) < n_prompt_tiles, pp_ref[0], ps_ref[0])
    pp = jnp.dot(ple.astype(BF16), wp_ref[0], preferred_element_type=F32)
    o_ref[...] = x2 + pg * pp


def _combine(dest_flat, ys, x1, gates_col, ple_prompt, ple_sample, vec1, wg_bf, wp_bf, layer):
    n = x1.shape[0]
    t = COMBINE_TILE
    n_prompt_tiles = ple_prompt.shape[1] // t
    assert ple_prompt.shape[1] % t == 0 and ple_sample.shape[1] % t == 0
    return pl.pallas_call(
        functools.partial(_combine_kernel, n_prompt_tiles=n_prompt_tiles),
        grid=(n // t,),
        in_specs=[pl.BlockSpec(memory_space=pl.ANY),
                  pl.BlockSpec(memory_space=pl.ANY),
                  pl.BlockSpec((t, D_MODEL), lambda i: (i, 0)),
                  pl.BlockSpec((t, TOP_K), lambda i: (i, 0)),
                  pl.BlockSpec((1, t, PLE_DIM), lambda i: (layer, jnp.minimum(i, n_prompt_tiles - 1), 0)),
                  pl.BlockSpec((1, t, PLE_DIM), lambda i: (layer, jnp.maximum(i - n_prompt_tiles, 0), 0)),
                  pl.BlockSpec((1, SUBLANES, D_MODEL), lambda i: (layer, 0, 0)),
                  pl.BlockSpec((1, D_MODEL, D_MODEL), lambda i: (layer, 0, 0)),
                  pl.BlockSpec((1, PLE_DIM, D_MODEL), lambda i: (layer, 0, 0))],
        out_specs=pl.BlockSpec((t, D_MODEL), lambda i: (i, 0)),
        out_shape=jax.ShapeDtypeStruct((n, D_MODEL), F32),
        scratch_shapes=[pltpu.SMEM((2, TOP_K * t), I32),
                        pltpu.VMEM((TOP_K, t, D_MODEL), F32),
                        pltpu.SemaphoreType.DMA((2,)), pltpu.SemaphoreType.DMA(())],
        compiler_params=_params("arbitrary"),
        name="combine",
    )(dest_flat, ys, x1, gates_col, ple_prompt, ple_sample, vec1, wg_bf, wp_bf)


def _block_diag_pairs(w_a, w_x):
    depth = w_a.shape[0]
    eye = jnp.eye(4, dtype=w_a.dtype)

    def halves(w):
        w = w.reshape(depth, 2, 4, LRU_BLOCK_DIM, LRU_BLOCK_DIM)
        return jnp.einsum('lhnij,nm->lhnimj', w, eye).reshape(depth, 2, 256, 256)

    return jnp.concatenate([halves(w_a), halves(w_x)], axis=-1).astype(BF16)


def kernel(x_prompt, x_sample, cache_k, cache_v, state_conv, state_h, p_prompt, p_sample, w_in, conv_w, conv_b,
           w_gate_a, b_gate_a, w_gate_x, b_gate_x, lru_lambda, g_lru_norm, g_attn_norm, w_out, ln1_g, ln1_b,
           w_router, b_router, w_up, b_up, w_down, b_down, ln2_g, ln2_b, w_ple_proj, w_ple_gate):
    depth = w_in.shape[0]
    batch, seq, _ = x_prompt.shape
    dec_batch, dec_seq, _ = x_sample.shape
    w_buf = cache_k.shape[2]
    n_prompt, n_sample = batch * seq, dec_batch * dec_seq
    n = n_prompt + n_sample
    assert dec_seq == SUBLANES and seq % LRU_CHUNK == 0 and n % TOKEN_TILE == 0 and n_prompt % TOKEN_TILE == 0

    w_in_bf = w_in.astype(BF16)
    w_out_bf = w_out.astype(BF16)
    wg_bf = w_ple_gate.astype(BF16)
    wp_bf = w_ple_proj.astype(BF16)
    wr_t = jnp.swapaxes(w_router, 1, 2).astype(BF16)
    b_router3 = b_router[:, :, None]
    w2 = _block_diag_pairs(w_gate_a, w_gate_x)
    zeros512 = jnp.zeros((depth, 7, LRU_WIDTH), F32)
    lru_p = jnp.concatenate([conv_w, conv_b[:, None], b_gate_a[:, None], b_gate_x[:, None], lru_lambda[:, None],
                             g_lru_norm[:, None], zeros512], axis=1)
    pad_attn = jnp.zeros((depth, D_MODEL - ATTN_WIDTH), F32)
    vec1 = jnp.stack([jnp.concatenate([g_attn_norm, pad_attn], axis=1), ln1_g, ln1_b, ln2_g, ln2_b,
                      jnp.zeros_like(ln1_g), jnp.zeros_like(ln1_g), jnp.zeros_like(ln1_g)], axis=1)
    b_up4 = b_up[:, :, None, :]
    b_down4 = b_down[:, :, None, :]
    tri = jnp.asarray(np.triu(np.ones((TOKEN_TILE, TOKEN_TILE), np.float32), k=1), BF16)
    bias_np, bias16_np = _prompt_bias_tables()
    bias_p, bias16_p = jnp.asarray(bias_np), jnp.asarray(bias16_np)
    sample_tables = tuple(jnp.asarray(t) for t in _sample_tables(w_buf, dec_seq))

    cs_rows = jnp.pad(state_conv, ((0, 0), (0, 0), (SUBLANES - (CONV_WIDTH - 1), 0), (0, 0))
                      ).reshape(depth, n_sample, LRU_WIDTH)
    h0_rows = jnp.repeat(state_h, dec_seq, axis=1)
    cs_zero = jnp.zeros((batch, SUBLANES, LRU_WIDTH), F32)
    ple_prompt = p_prompt.reshape(depth, n_prompt, PLE_DIM)
    ple_sample = p_sample.reshape(depth, n_sample, PLE_DIM)

    n_tiles = (n * TOP_K) // EXPERT_TILE + N_EXPERTS
    n_rows = n_tiles * EXPERT_TILE
    tile_start = jnp.arange(n_tiles, dtype=I32) * EXPERT_TILE
    assert n_rows % DISPATCH_TILE == 0
    expert_ids = jnp.arange(N_EXPERTS, dtype=I32)
    token_ids = jnp.arange(n * TOP_K, dtype=I32) // TOP_K
    pad_tokens = jnp.arange(n_rows, dtype=I32) % n

    x = jnp.concatenate([x_prompt.reshape(n_prompt, D_MODEL), x_sample.reshape(n_sample, D_MODEL)], axis=0)
    kp, vp, cp, hp, ks, vs, cs, hs = [], [], [], [], [], [], [], []
    for layer in range(depth):
        u, gate, q, k, v = _inproj(x, w_in_bf, layer)

        y_lru_p, h_last = _lru_prompt(u, gate, cs_zero, cs_zero, lru_p, w2, layer, batch, seq)
        y_lru_s, h_all = _lru_sample(u, gate, cs_rows[layer], h0_rows[layer], lru_p, w2,
                                     layer, n_prompt, n_sample)

        o_attn_p = _attn_prompt(q, k, v, bias_p, bias16_p, batch, seq)
        to_rows64 = lambda a: a[n_prompt:].reshape(n_sample * N_HEADS, HEAD_DIM)
        o_attn_s = _attn_sample(to_rows64(q), to_rows64(k), to_rows64(v), cache_k, cache_v, sample_tables,
                                layer, dec_batch, dec_seq, w_buf).reshape(n_sample, ATTN_WIDTH)

        x1, idx_t, gate_t, rank_t, cnt = _mix_route(x, y_lru_p, y_lru_s, o_attn_p, o_attn_s, w_out_bf, vec1,
                                                    wr_t, b_router3, tri, layer)

        counts = cnt[:, 0].astype(I32)
        padded = (counts + EXPERT_TILE - 1) // EXPERT_TILE * EXPERT_TILE
        ends = jnp.cumsum(padded)
        starts = ends - padded
        start_of = jnp.sum(jnp.where(idx_t[:TOP_K, :, None] == expert_ids, starts, 0), axis=-1)
        dest_flat = (start_of + rank_t[:TOP_K]).T.reshape(-1)
        tile_expert = jnp.minimum(jnp.sum((tile_start[:, None] >= ends[None, :]).astype(I32), axis=1),
                                  N_EXPERTS - 1)
        tile_valid = (tile_start < ends[-1]).astype(I32)
        tile_expert = jnp.where(tile_valid == 1, tile_expert,
                                jnp.max(jnp.where(tile_valid == 1, tile_expert, 0)))
        changed = jnp.concatenate([jnp.ones((1,), I32), (tile_expert[1:] != tile_expert[:-1]).astype(I32)])
        tile_first = changed * tile_valid

        row_token = pad_tokens.at[dest_flat].set(token_ids)
        xs = _dispatch(row_token, x1)
        ys = _experts(tile_expert, tile_first, tile_valid, xs, w_up, b_up4, w_down, b_down4, layer)
        x = _combine(dest_flat, ys, x1, gate_t[:TOP_K].T, ple_prompt, ple_sample, vec1, wg_bf, wp_bf, layer)

        k_p = k[:n_prompt].reshape(batch, seq, N_HEADS, HEAD_DIM)
        v_p = v[:n_prompt].reshape(batch, seq, N_HEADS, HEAD_DIM)
        w_prompt = min(DILATIONS[2] * ATTN_BLK, seq)
        kp.append(k_p[:, -w_prompt:])
        vp.append(v_p[:, -w_prompt:])
        cp.append(u[:n_prompt].reshape(batch, seq, LRU_WIDTH)[:, seq - (CONV_WIDTH - 1):])
        hp.append(h_last[:, 0])
        ks.append(k[n_prompt:].reshape(dec_batch, dec_seq, N_HEADS, HEAD_DIM))
        vs.append(v[n_prompt:].reshape(dec_batch, dec_seq, N_HEADS, HEAD_DIM))
        cs.append(u[n_prompt:].reshape(dec_batch, dec_seq, LRU_WIDTH)[:, dec_seq - (CONV_WIDTH - 1):])
        hs.append(h_all.reshape(dec_batch, dec_seq, LRU_WIDTH)[:, dec_seq - 1])

    y_prompt = x[:n_prompt].reshape(batch, seq, D_MODEL)
    y_sample = x[n_prompt:].reshape(dec_batch, dec_seq, D_MODEL)
    return (y_prompt, y_sample, jnp.stack(kp), jnp.stack(vp), jnp.stack(cp), jnp.stack(hp),
            jnp.stack(ks), jnp.stack(vs), jnp.stack(cs), jnp.stack(hs))
```

```python
import functools

import numpy as np
import jax
import jax.numpy as jnp
from jax import lax
from jax.experimental import pallas as pl
from jax.experimental.pallas import tpu as pltpu

F32 = jnp.float32
BF16 = jnp.bfloat16
I32 = jnp.int32

D_MODEL = 1024
LRU_WIDTH = 512
LRU_BLOCK_DIM = 64
CONV_WIDTH = 4
LRU_C = 8.0
N_HEADS = 8
HEAD_DIM = 64
ATTN_WIDTH = N_HEADS * HEAD_DIM
ATTN_SCALE = HEAD_DIM ** -0.5
ATTN_BLK = 128
DILATIONS = (1, 4, 16)
N_EXPERTS = 32
TOP_K = 4
D_FF = 1024
SWIGLU_ALPHA = 1.702
SWIGLU_LIMIT = 7.0
PLE_DIM = 256
MODEL_DEPTH = 4
DEEPNORM_ALPHA = (2.0 * MODEL_DEPTH) ** 0.25
LN_EPS = 1e-5
RMS_EPS = 1e-6

SUBLANES = 8
LANES = 128
VMEM_LIMIT_BYTES = 56 * 1024 * 1024

TOKEN_TILE = 512
LRU_CHUNK = 256
EXPERT_TILE = 512
COMBINE_TILE = 512
DISPATCH_TILE = 1024
ATTN_UNROLL = 4
DMA_UNROLL = 8


def _params(*sem):
    return pltpu.CompilerParams(dimension_semantics=sem, vmem_limit_bytes=VMEM_LIMIT_BYTES)


def _alibi_slopes():
    return [2.0 ** (-8.0 * (h + 1) / N_HEADS) for h in range(N_HEADS)]


def _inproj_kernel(x_ref, w_ref, u_ref, g_ref, q_ref, k_ref, v_ref):
    xb = x_ref[...].astype(BF16)
    for i, o_ref in enumerate((u_ref, g_ref, q_ref, k_ref, v_ref)):
        o_ref[...] = jnp.dot(xb, w_ref[0, :, i * 512:(i + 1) * 512], preferred_element_type=F32)


def _inproj(x, w_in_bf, layer):
    n = x.shape[0]
    out = jax.ShapeDtypeStruct((n, 512), F32)
    return pl.pallas_call(
        _inproj_kernel,
        grid=(n // TOKEN_TILE,),
        in_specs=[pl.BlockSpec((TOKEN_TILE, D_MODEL), lambda i: (i, 0)),
                  pl.BlockSpec((1, D_MODEL, 2560), lambda i: (layer, 0, 0))],
        out_specs=[pl.BlockSpec((TOKEN_TILE, 512), lambda i: (i, 0))] * 5,
        out_shape=[out] * 5,
        compiler_params=_params("parallel"),
        name="inproj",
    )(x, w_in_bf)


def _lru_coeffs(xc, p, w2_ref):
    xb = xc.astype(BF16)
    g0 = jnp.dot(xb[:, :256], w2_ref[0, 0], preferred_element_type=F32)
    g1 = jnp.dot(xb[:, 256:], w2_ref[0, 1], preferred_element_type=F32)
    ga = jnp.concatenate([g0[:, :256], g1[:, :256]], axis=1) + p[5:6, :]
    gx = jnp.concatenate([g0[:, 256:], g1[:, 256:]], axis=1) + p[6:7, :]
    r = jax.nn.sigmoid(ga)
    i = jax.nn.sigmoid(gx)
    z = -p[7:8, :]
    softplus = jnp.maximum(z, 0.0) + jnp.log1p(jnp.exp(-jnp.abs(z)))
    log_a = (-LRU_C) * r * softplus
    a = jnp.exp(log_a)
    b = jnp.sqrt(-jnp.tanh(log_a) * (a * a + 1.0)) * i * xc
    return a, b


def _group_scan(a, b):
    row = lax.broadcasted_iota(I32, a.shape, 0) & (SUBLANES - 1)
    for s in (1, 2, 4):
        a_prev = pltpu.roll(a, s, 0)
        b_prev = pltpu.roll(b, s, 0)
        m = row >= s
        b = jnp.where(m, a * b_prev + b, b)
        a = jnp.where(m, a * a_prev, a)
    return a, b


def _lru_finish(h, gate, p):
    y = h * jax.nn.gelu(gate)
    var = jnp.mean(y * y, axis=-1, keepdims=True)
    return (y * lax.rsqrt(var + RMS_EPS) * p[8:9, :]).astype(BF16)


def _lru_prompt_kernel(u_ref, gate_ref, cs_ref, h0_ref, p_ref, w2_ref, y_ref, hlast_ref, tail_s, hc_s):
    c = pl.program_id(1)

    @pl.when(c == 0)
    def _():
        tail_s[...] = cs_ref[0]
        hc_s[...] = h0_ref[0]

    p = p_ref[0]
    u = u_ref[...]
    tail = tail_s[...]
    row8 = lax.broadcasted_iota(I32, (SUBLANES, LRU_WIDTH), 0)
    xc = p[4:5, :] + p[3:4, :] * u
    for s in (1, 2, 3):
        sh = pltpu.roll(u, s, 0)
        first = jnp.where(row8 < s, pltpu.roll(tail, s, 0), sh[0:SUBLANES])
        xc = xc + p[3 - s:4 - s, :] * jnp.concatenate([first, sh[SUBLANES:]], axis=0)
    tail_s[...] = u[LRU_CHUNK - SUBLANES:]

    a, b = _lru_coeffs(xc, p, w2_ref)
    a, b = _group_scan(a, b)
    h = hc_s[0:1, :]
    hs = []
    for g in range(LRU_CHUNK // SUBLANES):
        sl = slice(g * SUBLANES, (g + 1) * SUBLANES)
        hg = a[sl] * h + b[sl]
        hs.append(hg)
        h = hg[SUBLANES - 1:SUBLANES, :]
    hc_s[...] = jnp.broadcast_to(h, (SUBLANES, LRU_WIDTH))
    hlast_ref[0] = jnp.broadcast_to(h, (SUBLANES, LRU_WIDTH))
    y_ref[...] = _lru_finish(jnp.concatenate(hs, axis=0), gate_ref[...], p)


def _lru_prompt(u, gate, cs8, h08, lru_p, w2, layer, batch, seq):
    n_chunks = seq // LRU_CHUNK
    n_total = batch * seq
    return pl.pallas_call(
        _lru_prompt_kernel,
        grid=(batch, n_chunks),
        in_specs=[pl.BlockSpec((LRU_CHUNK, LRU_WIDTH), lambda b, c: (b * n_chunks + c, 0)),
                  pl.BlockSpec((LRU_CHUNK, LRU_WIDTH), lambda b, c: (b * n_chunks + c, 0)),
                  pl.BlockSpec((1, SUBLANES, LRU_WIDTH), lambda b, c: (b, 0, 0)),
                  pl.BlockSpec((1, SUBLANES, LRU_WIDTH), lambda b, c: (b, 0, 0)),
                  pl.BlockSpec((1, 16, LRU_WIDTH), lambda b, c: (layer, 0, 0)),
                  pl.BlockSpec((1, 2, 256, 512), lambda b, c: (layer, 0, 0, 0))],
        out_specs=[pl.BlockSpec((LRU_CHUNK, LRU_WIDTH), lambda b, c: (b * n_chunks + c, 0)),
                   pl.BlockSpec((1, SUBLANES, LRU_WIDTH), lambda b, c: (b, 0, 0))],
        out_shape=[jax.ShapeDtypeStruct((n_total, LRU_WIDTH), BF16),
                   jax.ShapeDtypeStruct((batch, SUBLANES, LRU_WIDTH), F32)],
        scratch_shapes=[pltpu.VMEM((SUBLANES, LRU_WIDTH), F32), pltpu.VMEM((SUBLANES, LRU_WIDTH), F32)],
        compiler_params=_params("arbitrary", "arbitrary"),
        name="lru_prompt",
    )(u, gate, cs8, h08, lru_p, w2)


def _lru_sample_kernel(u_ref, gate_ref, cs_ref, h0_ref, p_ref, w2_ref, y_ref, h_ref):
    p = p_ref[0]
    u = u_ref[...]
    cs = cs_ref[...]
    rows = u.shape[0]
    row = lax.broadcasted_iota(I32, u.shape, 0) & (SUBLANES - 1)
    xc = p[4:5, :] + p[3:4, :] * u
    for s in (1, 2, 3):
        sh = pltpu.roll(u, s, 0)
        cs_sh = pltpu.roll(cs, rows - SUBLANES + s, 0)
        xc = xc + p[3 - s:4 - s, :] * jnp.where(row < s, cs_sh, sh)
    a, b = _lru_coeffs(xc, p, w2_ref)
    a, b = _group_scan(a, b)
    h = a * h0_ref[...] + b
    h_ref[...] = h
    y_ref[...] = _lru_finish(h, gate_ref[...], p)


def _lru_sample(u, gate, cs_rows, h0_rows, lru_p, w2, layer, n_prompt, n_sample):
    tile = min(256, n_sample)
    off = n_prompt // tile
    return pl.pallas_call(
        _lru_sample_kernel,
        grid=(n_sample // tile,),
        in_specs=[pl.BlockSpec((tile, LRU_WIDTH), lambda i: (off + i, 0)),
                  pl.BlockSpec((tile, LRU_WIDTH), lambda i: (off + i, 0)),
                  pl.BlockSpec((tile, LRU_WIDTH), lambda i: (i, 0)),
                  pl.BlockSpec((tile, LRU_WIDTH), lambda i: (i, 0)),
                  pl.BlockSpec((1, 16, LRU_WIDTH), lambda i: (layer, 0, 0)),
                  pl.BlockSpec((1, 2, 256, 512), lambda i: (layer, 0, 0, 0))],
        out_specs=[pl.BlockSpec((tile, LRU_WIDTH), lambda i: (i, 0)),
                   pl.BlockSpec((tile, LRU_WIDTH), lambda i: (i, 0))],
        out_shape=[jax.ShapeDtypeStruct((n_sample, LRU_WIDTH), BF16),
                   jax.ShapeDtypeStruct((n_sample, LRU_WIDTH), F32)],
        compiler_params=_params("parallel"),
        name="lru_sample",
    )(u, gate, cs_rows, h0_rows, lru_p, w2)


def _prompt_bias_tables():
    slopes = _alibi_slopes()
    q = np.arange(ATTN_BLK)[:, None]
    k = np.arange(2 * ATTN_BLK)[None, :]
    steps = q + ATTN_BLK - k
    valid = (steps >= 0) & (steps <= ATTN_BLK)
    steps_f = q - k
    valid_f = (k < ATTN_BLK) & (steps_f >= 0)
    table = np.full((N_HEADS, 2, 2, ATTN_BLK, 2 * ATTN_BLK), -np.inf, np.float32)
    table16 = np.full((N_HEADS, ATTN_BLK, ATTN_BLK), -np.inf, np.float32)
    for h in range(N_HEADS):
        for di, d in enumerate(DILATIONS[:2]):
            table[h, di, 0] = np.where(valid, -slopes[h] * d * steps, -np.inf)
            table[h, di, 1] = np.where(valid_f, -slopes[h] * d * steps_f, -np.inf)
        s16 = q - k[:, :ATTN_BLK]
        table16[h] = np.where(s16 >= 0, -slopes[h] * DILATIONS[2] * s16, -np.inf)
    return table, table16


def _attn_prompt_kernel(q_ref, k_ref, v_ref, bias_ref, bias16_ref, o_ref, acc_s, m_s, l_s, *, seq):
    lane = lax.broadcasted_iota(I32, (ATTN_BLK, LANES), 1)
    lo = lane < HEAD_DIM

    def rows(start, size, stride):
        return pl.ds(start, size) if stride == 1 else pl.ds(start, size, stride=stride)

    def tile(di, qstart, kstart, nk, stride, bias_of_head):
        qsl = rows(qstart, ATTN_BLK, stride)
        ksl = rows(kstart, nk, stride)
        qs = q_ref[qsl, :] * ATTN_SCALE
        kk = k_ref[ksl, :].astype(BF16)
        vv = v_ref[ksl, :].astype(BF16)
        parts = []
        for hh in (0, 1):
            qm = jnp.where(lo if hh == 0 else jnp.logical_not(lo), qs, 0.0).astype(BF16)
            s = lax.dot_general(qm, kk, (((1,), (1,)), ((), ())), preferred_element_type=F32)
            s = s + bias_of_head(hh)
            m = jnp.max(s, axis=-1, keepdims=True)
            e = jnp.exp(s - m)
            l = jnp.sum(e, axis=-1, keepdims=True)
            pv = jnp.dot(e.astype(BF16), vv, preferred_element_type=F32)
            parts.append((pv, m, l))
        (pv0, m0, l0), (pv1, m1, l1) = parts
        acc_s[di, qsl, :] = jnp.where(lo, pv0, pv1)
        m_s[di, qsl, :] = jnp.where(lo, m0, m1)
        l_s[di, qsl, :] = jnp.where(lo, l0, l1)

    d1, d4, d16 = DILATIONS
    span4 = d4 * ATTN_BLK

    def body1(it, carry):
        for u in range(ATTN_UNROLL):
            j = it * ATTN_UNROLL + u
            qstart = pl.multiple_of(ATTN_BLK * j, ATTN_BLK)
            if u == 0:
                first = 1 - jnp.minimum(j, 1)
                kstart = pl.multiple_of(ATTN_BLK * jnp.maximum(j - 1, 0), ATTN_BLK)
            else:
                first = 0
                kstart = pl.multiple_of(ATTN_BLK * (j - 1), ATTN_BLK)
            tile(0, qstart, kstart, 2 * ATTN_BLK, d1, lambda hh, first=first: bias_ref[hh, 0, first])
        return carry

    lax.fori_loop(0, seq // (ATTN_BLK * ATTN_UNROLL), body1, 0)

    def body4(r, carry):
        for j in range(seq // span4):
            tile(1, r + span4 * j, r + span4 * max(j - 1, 0), 2 * ATTN_BLK, d4,
                 lambda hh, first=int(j == 0): bias_ref[hh, 1, first])
        return carry

    lax.fori_loop(0, d4, body4, 0)

    def body16(it, carry):
        for u in range(ATTN_UNROLL):
            r = it * ATTN_UNROLL + u
            tile(2, r, r, ATTN_BLK, d16, lambda hh: bias16_ref[hh])
        return carry

    lax.fori_loop(0, d16 // ATTN_UNROLL, body16, 0)

    chunk = 256

    def merge(ci, carry):
        sl = pl.ds(pl.multiple_of(ci * chunk, chunk), chunk)
        m0, m1, m2 = m_s[0, sl, :], m_s[1, sl, :], m_s[2, sl, :]
        mm = jnp.maximum(jnp.maximum(m0, m1), m2)
        w0, w1, w2 = jnp.exp(m0 - mm), jnp.exp(m1 - mm), jnp.exp(m2 - mm)
        den = w0 * l_s[0, sl, :] + w1 * l_s[1, sl, :] + w2 * l_s[2, sl, :]
        num = w0 * acc_s[0, sl, :] + w1 * acc_s[1, sl, :] + w2 * acc_s[2, sl, :]
        o_ref[sl, :] = num / den
        return carry

    lax.fori_loop(0, seq // chunk, merge, 0)


def _attn_prompt(q, k, v, bias, bias16, batch, seq):
    assert seq == DILATIONS[2] * ATTN_BLK, "dilation-16 pass assumes one tile per residue class"
    hp = N_HEADS // 2
    n_total = batch * seq
    qkv_spec = pl.BlockSpec((seq, LANES), lambda b, p: (b, p))
    return pl.pallas_call(
        functools.partial(_attn_prompt_kernel, seq=seq),
        grid=(batch, hp),
        in_specs=[qkv_spec, qkv_spec, qkv_spec,
                  pl.BlockSpec((2, 2, 2, ATTN_BLK, 2 * ATTN_BLK), lambda b, p: (p, 0, 0, 0, 0)),
                  pl.BlockSpec((2, ATTN_BLK, ATTN_BLK), lambda b, p: (p, 0, 0))],
        out_specs=pl.BlockSpec((seq, LANES), lambda b, p: (b, p)),
        out_shape=jax.ShapeDtypeStruct((n_total, ATTN_WIDTH), F32),
        scratch_shapes=[pltpu.VMEM((3, seq, LANES), F32)] * 3,
        compiler_params=_params("parallel", "parallel"),
        name="attn_prompt",
    )(q, k, v, bias, bias16)


def _sample_tables(w_buf, dec_seq):
    slopes = np.asarray(_alibi_slopes())
    rows = np.arange(N_HEADS * dec_seq)
    h, t = rows // dec_seq, rows % dec_seq

    def table(dist, real):
        cnt = np.zeros(dist.shape, np.float64)
        for d in DILATIONS:
            cnt += real & (dist >= 0) & (dist % d == 0) & (dist <= d * ATTN_BLK)
        return np.where(cnt > 0, -slopes[h][:, None] * dist + np.log(np.maximum(cnt, 1.0)),
                        -np.inf).astype(np.float32)

    keys = np.arange(w_buf)[None, :]
    own = np.arange(LANES)[None, :]
    return (table(w_buf + t[:, None] - keys, keys >= 0),
            table(t[:, None] - own, own < dec_seq))


def _attn_sample_kernel(q_ref, k_ref, v_ref, ckt_ref, cvt_ref, bias_ref, biasn_ref, o_ref, *, dec_seq):
    n_rows = N_HEADS * dec_seq
    w_buf = ckt_ref.shape[-1]
    q = q_ref[...] * ATTN_SCALE
    qt = jnp.concatenate([q] * N_HEADS, axis=0)
    own = (lax.broadcasted_iota(I32, (n_rows, ATTN_WIDTH), 0) // dec_seq
           == lax.broadcasted_iota(I32, (n_rows, ATTN_WIDTH), 1) // HEAD_DIM)
    qm = jnp.where(own, qt, 0.0).astype(BF16)
    nt = (((1,), (1,)), ((), ()))
    kct = ckt_ref[0, 0].reshape(ATTN_WIDTH, w_buf).astype(BF16)
    s_c = jnp.dot(qm, kct, preferred_element_type=F32) + bias_ref[...]
    pad = jnp.zeros((LANES - dec_seq, ATTN_WIDTH), F32)
    kn = jnp.concatenate([k_ref[...], pad], axis=0).astype(BF16)
    vn = jnp.concatenate([v_ref[...], pad], axis=0).astype(BF16)
    s_n = lax.dot_general(qm, kn, nt, preferred_element_type=F32) + biasn_ref[...]
    m = jnp.maximum(jnp.max(s_c, axis=-1, keepdims=True), jnp.max(s_n, axis=-1, keepdims=True))
    e_c = jnp.exp(s_c - m)
    e_n = jnp.exp(s_n - m)
    l = jnp.sum(e_c, axis=-1, keepdims=True) + jnp.sum(e_n, axis=-1, keepdims=True)
    vct = cvt_ref[0, 0].reshape(ATTN_WIDTH, w_buf).astype(BF16)
    r = (lax.dot_general(e_c.astype(BF16), vct, nt, preferred_element_type=F32)
         + jnp.dot(e_n.astype(BF16), vn, preferred_element_type=F32))
    r = jnp.where(own, r, 0.0) / l
    o = r[0:dec_seq]
    for h in range(1, N_HEADS):
        o = o + r[h * dec_seq:(h + 1) * dec_seq]
    o_ref[...] = o


def _attn_sample(q, k, v, cache_kt, cache_vt, tables, layer, n_prompt, dec_batch, dec_seq):
    w_buf = cache_kt.shape[-1]
    off = n_prompt // dec_seq
    n_rows = N_HEADS * dec_seq
    row_spec = pl.BlockSpec((dec_seq, ATTN_WIDTH), lambda b: (off + b, 0))
    cache_spec = pl.BlockSpec((1, 1, N_HEADS, HEAD_DIM, w_buf), lambda b: (layer, b, 0, 0, 0))
    bias, bias_n = tables
    return pl.pallas_call(
        functools.partial(_attn_sample_kernel, dec_seq=dec_seq),
        grid=(dec_batch,),
        in_specs=[row_spec, row_spec, row_spec, cache_spec, cache_spec,
                  pl.BlockSpec((n_rows, w_buf), lambda b: (0, 0)),
                  pl.BlockSpec((n_rows, LANES), lambda b: (0, 0))],
        out_specs=pl.BlockSpec((dec_seq, ATTN_WIDTH), lambda b: (b, 0)),
        out_shape=jax.ShapeDtypeStruct((dec_batch * dec_seq, ATTN_WIDTH), F32),
        compiler_params=_params("parallel"),
        name="attn_sample",
    )(q, k, v, cache_kt, cache_vt, bias, bias_n)


def _layer_norm(z, g, b):
    mu = jnp.mean(z, axis=-1, keepdims=True)
    zc = z - mu
    var = jnp.mean(zc * zc, axis=-1, keepdims=True)
    return zc * lax.rsqrt(var + LN_EPS) * g + b


def _mix_route_kernel(x_ref, ylp_ref, yls_ref, oap_ref, oas_ref, wout_ref, vec_ref, wr_ref, br_ref, tri_ref,
                      x1_ref, idx_ref, gate_ref, rank_ref, cnt_ref, carry_s, *, n_prompt_tiles):
    i = pl.program_id(0)

    @pl.when(i == 0)
    def _():
        carry_s[...] = jnp.zeros_like(carry_s)

    vec = vec_ref[0]
    is_prompt = i < n_prompt_tiles
    oa = jnp.where(is_prompt, oap_ref[...], oas_ref[...])
    yl = jnp.where(is_prompt, ylp_ref[...], yls_ref[...])
    oa_n = (oa * lax.rsqrt(jnp.mean(oa * oa, axis=-1, keepdims=True) + RMS_EPS) * vec[0:1, :ATTN_WIDTH])
    mixed = (jnp.dot(yl, wout_ref[0, :LRU_WIDTH], preferred_element_type=F32)
             + jnp.dot(oa_n.astype(BF16), wout_ref[0, LRU_WIDTH:], preferred_element_type=F32))
    x1 = _layer_norm(DEEPNORM_ALPHA * x_ref[...] + mixed, vec[1:2, :], vec[2:3, :])
    x1_ref[...] = x1

    logits = lax.dot_general(wr_ref[0], x1.astype(BF16), (((1,), (1,)), ((), ())),
                             preferred_element_type=F32) + br_ref[0]
    tile = logits.shape[1]
    e_iota = lax.broadcasted_iota(I32, logits.shape, 0).astype(F32)
    work = logits
    vals, idxs, sels = [], [], []
    for _ in range(TOP_K):
        mx = jnp.max(work, axis=0, keepdims=True)
        idx = jnp.min(jnp.where(work == mx, e_iota, float(N_EXPERTS)), axis=0, keepdims=True)
        sel = e_iota == idx
        vals.append(mx)
        idxs.append(idx)
        sels.append(sel)
        work = jnp.where(sel, -jnp.inf, work)
    exps = [jnp.exp(v - vals[0]) for v in vals]
    den = exps[0] + exps[1] + exps[2] + exps[3]
    gates = [e / den for e in exps]

    onehot = jnp.zeros(logits.shape, F32)
    for sel in sels:
        onehot = onehot + sel.astype(F32)
    prefix = jnp.dot(onehot.astype(BF16), tri_ref[...], preferred_element_type=F32) + carry_s[:, 0:1]
    ranks = [jnp.sum(jnp.where(sel, prefix, 0.0), axis=0, keepdims=True) for sel in sels]
    carry = carry_s[:, 0:1] + jnp.sum(onehot, axis=1, keepdims=True)
    carry_s[...] = jnp.broadcast_to(carry, carry_s.shape)
    cnt_ref[...] = jnp.broadcast_to(carry, cnt_ref.shape)

    row = lax.broadcasted_iota(I32, (SUBLANES, tile), 0)

    def stack(parts):
        out = jnp.zeros((SUBLANES, tile), F32)
        for k_, part in enumerate(parts):
            out = jnp.where(row == k_, part, out)
        return out

    idx_ref[...] = stack(idxs).astype(I32)
    gate_ref[...] = stack(gates)
    rank_ref[...] = stack(ranks).astype(I32)


def _mix_route(x, y_lru_p, y_lru_s, o_attn_p, o_attn_s, w_out_bf, vec1, wr_t, b_router, tri, layer):
    n = x.shape[0]
    t = TOKEN_TILE
    n_prompt_tiles = y_lru_p.shape[0] // t
    assert y_lru_p.shape[0] % t == 0 and y_lru_s.shape[0] % t == 0
    row = lambda w: pl.BlockSpec((t, w), lambda i: (i, 0))
    prompt_row = lambda w: pl.BlockSpec((t, w), lambda i: (jnp.minimum(i, n_prompt_tiles - 1), 0))
    sample_row = lambda w: pl.BlockSpec((t, w), lambda i: (jnp.maximum(i - n_prompt_tiles, 0), 0))
    col = pl.BlockSpec((SUBLANES, t), lambda i: (0, i))
    return pl.pallas_call(
        functools.partial(_mix_route_kernel, n_prompt_tiles=n_prompt_tiles),
        grid=(n // t,),
        in_specs=[row(D_MODEL), prompt_row(LRU_WIDTH), sample_row(LRU_WIDTH),
                  prompt_row(ATTN_WIDTH), sample_row(ATTN_WIDTH),
                  pl.BlockSpec((1, D_MODEL, D_MODEL), lambda i: (layer, 0, 0)),
                  pl.BlockSpec((1, SUBLANES, D_MODEL), lambda i: (layer, 0, 0)),
                  pl.BlockSpec((1, N_EXPERTS, D_MODEL), lambda i: (layer, 0, 0)),
                  pl.BlockSpec((1, N_EXPERTS, 1), lambda i: (layer, 0, 0)),
                  pl.BlockSpec((t, t), lambda i: (0, 0))],
        out_specs=[row(D_MODEL), col, col, col,
                   pl.BlockSpec((N_EXPERTS, LANES), lambda i: (0, 0))],
        out_shape=[jax.ShapeDtypeStruct((n, D_MODEL), F32),
                   jax.ShapeDtypeStruct((SUBLANES, n), I32),
                   jax.ShapeDtypeStruct((SUBLANES, n), F32),
                   jax.ShapeDtypeStruct((SUBLANES, n), I32),
                   jax.ShapeDtypeStruct((N_EXPERTS, LANES), F32)],
        scratch_shapes=[pltpu.VMEM((N_EXPERTS, LANES), F32)],
        compiler_params=_params("arbitrary"),
        name="mix_route",
    )(x, y_lru_p, y_lru_s, o_attn_p, o_attn_s, w_out_bf, vec1, wr_t, b_router, tri)


def _fetch_indices(dest_hbm, idx_s, idx_sem, n_idx):
    i = pl.program_id(0)
    slot = i % 2

    def idx_copy(step, sl):
        return pltpu.make_async_copy(dest_hbm.at[pl.ds(pl.multiple_of(step * n_idx, n_idx), n_idx)],
                                     idx_s.at[pl.ds(pl.multiple_of(sl * n_idx, n_idx), n_idx)], idx_sem.at[sl])

    @pl.when(i == 0)
    def _():
        idx_copy(0, 0).start()

    idx_copy(i, slot).wait()

    @pl.when(i + 1 < pl.num_programs(0))
    def _():
        idx_copy(i + 1, 1 - slot).start()

    return slot * n_idx


def _dispatch_kernel(tok_hbm, x_hbm, xs_ref, idx_s, idx_sem, sem):
    off = _fetch_indices(tok_hbm, idx_s, idx_sem, DISPATCH_TILE)

    def row_copy(it, j):
        token = idx_s[off + it * SUBLANES + j]
        return pltpu.make_async_copy(x_hbm.at[pl.ds(token, 1), :], xs_ref.at[it, pl.ds(j, 1), :], sem)

    def issue(it, carry):
        for j in range(SUBLANES):
            row_copy(it, j).start(priority=j % 2)
        return carry

    lax.fori_loop(0, DISPATCH_TILE // SUBLANES, issue, 0)

    def drain(it, carry):
        for j in range(SUBLANES):
            row_copy(it, j).wait()
        return carry

    lax.fori_loop(0, DISPATCH_TILE // SUBLANES, drain, 0)


def _dispatch(row_token, x1):
    n_rows = row_token.shape[0]
    xs = pl.pallas_call(
        _dispatch_kernel,
        grid=(n_rows // DISPATCH_TILE,),
        in_specs=[pl.BlockSpec(memory_space=pl.ANY), pl.BlockSpec(memory_space=pl.ANY)],
        out_specs=pl.BlockSpec((DISPATCH_TILE // SUBLANES, SUBLANES, D_MODEL), lambda i: (i, 0, 0)),
        out_shape=jax.ShapeDtypeStruct((n_rows // SUBLANES, SUBLANES, D_MODEL), F32),
        scratch_shapes=[pltpu.SMEM((2 * DISPATCH_TILE,), I32),
                        pltpu.SemaphoreType.DMA((2,)), pltpu.SemaphoreType.DMA(())],
        compiler_params=_params("arbitrary"),
        name="dispatch",
    )(row_token, x1)
    return xs.reshape(n_rows, D_MODEL)


def _expert_kernel(te_ref, tf_ref, tv_ref, xs_ref, wu_ref, bu_ref, wd_ref, bd_ref, ys_ref, wub_s, wdb_s):
    del te_ref
    i = pl.program_id(0)

    @pl.when(tf_ref[i] == 1)
    def _():
        wub_s[...] = wu_ref[0, 0].astype(BF16)
        wdb_s[...] = wd_ref[0, 0].astype(BF16)

    @pl.when(tv_ref[i] == 0)
    def _():
        ys_ref[...] = jnp.zeros_like(ys_ref)

    @pl.when(tv_ref[i] == 1)
    def _():
        xb = xs_ref[...].astype(BF16)
        h = jnp.dot(xb, wub_s[...], preferred_element_type=F32) + bu_ref[0, 0]
        glu = jnp.minimum(h[:, :D_FF], SWIGLU_LIMIT)
        lin = jnp.clip(h[:, D_FF:], -SWIGLU_LIMIT, SWIGLU_LIMIT)
        act = glu * jax.nn.sigmoid(SWIGLU_ALPHA * glu) * (lin + 1.0)
        ys_ref[...] = jnp.dot(act.astype(BF16), wdb_s[...], preferred_element_type=F32) + bd_ref[0, 0]


def _experts(tile_expert, tile_first, tile_valid, xs, w_up, b_up4, w_down, b_down4, layer):
    n_rows = xs.shape[0]
    n_tiles = n_rows // EXPERT_TILE
    grid_spec = pltpu.PrefetchScalarGridSpec(
        num_scalar_prefetch=3,
        grid=(n_tiles,),
        in_specs=[pl.BlockSpec((EXPERT_TILE, D_MODEL), lambda i, te, tf, tv: (i, 0)),
                  pl.BlockSpec((1, 1, D_MODEL, 2 * D_FF), lambda i, te, tf, tv: (layer, te[i], 0, 0)),
                  pl.BlockSpec((1, 1, 1, 2 * D_FF), lambda i, te, tf, tv: (layer, te[i], 0, 0)),
                  pl.BlockSpec((1, 1, D_FF, D_MODEL), lambda i, te, tf, tv: (layer, te[i], 0, 0)),
                  pl.BlockSpec((1, 1, 1, D_MODEL), lambda i, te, tf, tv: (layer, te[i], 0, 0))],
        out_specs=pl.BlockSpec((EXPERT_TILE, D_MODEL), lambda i, te, tf, tv: (i, 0)),
        scratch_shapes=[pltpu.VMEM((D_MODEL, 2 * D_FF), BF16), pltpu.VMEM((D_FF, D_MODEL), BF16)],
    )
    return pl.pallas_call(
        _expert_kernel,
        grid_spec=grid_spec,
        out_shape=jax.ShapeDtypeStruct((n_rows, D_MODEL), F32),
        compiler_params=_params("arbitrary"),
        name="experts",
    )(tile_expert, tile_first, tile_valid, xs, w_up, b_up4, w_down, b_down4)


def _combine_kernel(dest_hbm, ys_hbm, x1_ref, gate_ref, pp_ref, ps_ref, vec_ref, wg_ref, wp_ref, o_ref,
                    idx_s, buf_s, idx_sem, sem, *, n_prompt_tiles):
    off = _fetch_indices(dest_hbm, idx_s, idx_sem, TOP_K * COMBINE_TILE)

    def row_copy(it, j, k):
        row = idx_s[off + (it * SUBLANES + j) * TOP_K + k]
        return pltpu.make_async_copy(ys_hbm.at[pl.ds(row, 1), :], buf_s.at[k, it, pl.ds(j, 1), :], sem)

    def issue(it, carry):
        for j in range(SUBLANES):
            for k in range(TOP_K):
                row_copy(it, j, k).start(priority=k % 2)
        return carry

    lax.fori_loop(0, COMBINE_TILE // SUBLANES, issue, 0)

    def drain(it, carry):
        for j in range(SUBLANES):
            for k in range(TOP_K):
                row_copy(it, j, k).wait()
        return carry

    lax.fori_loop(0, COMBINE_TILE // SUBLANES, drain, 0)

    vec = vec_ref[0]
    g = gate_ref[...]
    rows = lambda k: buf_s[k].reshape(COMBINE_TILE, D_MODEL)
    ffn = g[:, 0:1] * rows(0)
    for k in range(1, TOP_K):
        ffn = ffn + g[:, k:k + 1] * rows(k)
    x2 = _layer_norm(DEEPNORM_ALPHA * x1_ref[...] + ffn, vec[3:4, :], vec[4:5, :])
    pg = jax.nn.sigmoid(jnp.dot(x2.astype(BF16), wg_ref[0], preferred_element_type=F32))
    ple = jnp.where(pl.program_id(0) < n_prompt_tiles, pp_ref[0], ps_ref[0])
    pp = jnp.dot(ple.astype(BF16), wp_ref[0], preferred_element_type=F32)
    o_ref[...] = x2 + pg * pp


def _combine(dest_flat, ys, x1, gates_col, ple_prompt, ple_sample, vec1, wg_bf, wp_bf, layer):
    n = x1.shape[0]
    t = COMBINE_TILE
    n_prompt_tiles = ple_prompt.shape[1] // t
    assert ple_prompt.shape[1] % t == 0 and ple_sample.shape[1] % t == 0
    return pl.pallas_call(
        functools.partial(_combine_kernel, n_prompt_tiles=n_prompt_tiles),
        grid=(n // t,),
        in_specs=[pl.BlockSpec(memory_space=pl.ANY),
                  pl.BlockSpec(memory_space=pl.ANY),
                  pl.BlockSpec((t, D_MODEL), lambda i: (i, 0)),
                  pl.BlockSpec((t, TOP_K), lambda i: (i, 0)),
                  pl.BlockSpec((1, t, PLE_DIM), lambda i: (layer, jnp.minimum(i, n_prompt_tiles - 1), 0)),
                  pl.BlockSpec((1, t, PLE_DIM), lambda i: (layer, jnp.maximum(i - n_prompt_tiles, 0), 0)),
                  pl.BlockSpec((1, SUBLANES, D_MODEL), lambda i: (layer, 0, 0)),
                  pl.BlockSpec((1, D_MODEL, D_MODEL), lambda i: (layer, 0, 0)),
                  pl.BlockSpec((1, PLE_DIM, D_MODEL), lambda i: (layer, 0, 0))],
        out_specs=pl.BlockSpec((t, D_MODEL), lambda i: (i, 0)),
        out_shape=jax.ShapeDtypeStruct((n, D_MODEL), F32),
        scratch_shapes=[pltpu.SMEM((2 * TOP_K * t,), I32),
                        pltpu.VMEM((TOP_K, t // SUBLANES, SUBLANES, D_MODEL), F32),
                        pltpu.SemaphoreType.DMA((2,)), pltpu.SemaphoreType.DMA(())],
        compiler_params=_params("arbitrary"),
        name="combine",
    )(dest_flat, ys, x1, gates_col, ple_prompt, ple_sample, vec1, wg_bf, wp_bf)


def _block_diag_pairs(w_a, w_x):
    depth = w_a.shape[0]
    eye = jnp.eye(4, dtype=w_a.dtype)

    def halves(w):
        w = w.reshape(depth, 2, 4, LRU_BLOCK_DIM, LRU_BLOCK_DIM)
        return jnp.einsum('lhnij,nm->lhnimj', w, eye).reshape(depth, 2, 256, 256)

    return jnp.concatenate([halves(w_a), halves(w_x)], axis=-1).astype(BF16)


def kernel(x_prompt, x_sample, cache_k, cache_v, state_conv, state_h, p_prompt, p_sample, w_in, conv_w, conv_b,
           w_gate_a, b_gate_a, w_gate_x, b_gate_x, lru_lambda, g_lru_norm, g_attn_norm, w_out, ln1_g, ln1_b,
           w_router, b_router, w_up, b_up, w_down, b_down, ln2_g, ln2_b, w_ple_proj, w_ple_gate):
    depth = w_in.shape[0]
    batch, seq, _ = x_prompt.shape
    dec_batch, dec_seq, _ = x_sample.shape
    w_buf = cache_k.shape[2]
    n_prompt, n_sample = batch * seq, dec_batch * dec_seq
    n = n_prompt + n_sample
    assert dec_seq == SUBLANES and seq % LRU_CHUNK == 0 and n % TOKEN_TILE == 0 and n_prompt % TOKEN_TILE == 0

    w_in_bf = w_in.astype(BF16)
    w_out_bf = w_out.astype(BF16)
    wg_bf = w_ple_gate.astype(BF16)
    wp_bf = w_ple_proj.astype(BF16)
    wr_t = jnp.swapaxes(w_router, 1, 2).astype(BF16)
    b_router3 = b_router[:, :, None]
    w2 = _block_diag_pairs(w_gate_a, w_gate_x)
    zeros512 = jnp.zeros((depth, 7, LRU_WIDTH), F32)
    lru_p = jnp.concatenate([conv_w, conv_b[:, None], b_gate_a[:, None], b_gate_x[:, None], lru_lambda[:, None],
                             g_lru_norm[:, None], zeros512], axis=1)
    pad_attn = jnp.zeros((depth, D_MODEL - ATTN_WIDTH), F32)
    vec1 = jnp.stack([jnp.concatenate([g_attn_norm, pad_attn], axis=1), ln1_g, ln1_b, ln2_g, ln2_b,
                      jnp.zeros_like(ln1_g), jnp.zeros_like(ln1_g), jnp.zeros_like(ln1_g)], axis=1)
    b_up4 = b_up[:, :, None, :]
    b_down4 = b_down[:, :, None, :]
    tri = jnp.asarray(np.triu(np.ones((TOKEN_TILE, TOKEN_TILE), np.float32), k=1), BF16)
    bias_np, bias16_np = _prompt_bias_tables()
    bias_p, bias16_p = jnp.asarray(bias_np), jnp.asarray(bias16_np)
    sample_tables = tuple(jnp.asarray(t) for t in _sample_tables(w_buf, dec_seq))
    cache_kt = jnp.transpose(cache_k, (0, 1, 3, 4, 2))
    cache_vt = jnp.transpose(cache_v, (0, 1, 3, 4, 2))

    cs_rows = jnp.pad(state_conv, ((0, 0), (0, 0), (SUBLANES - (CONV_WIDTH - 1), 0), (0, 0))
                      ).reshape(depth, n_sample, LRU_WIDTH)
    h0_rows = jnp.repeat(state_h, dec_seq, axis=1)
    cs_zero = jnp.zeros((batch, SUBLANES, LRU_WIDTH), F32)
    ple_prompt = p_prompt.reshape(depth, n_prompt, PLE_DIM)
    ple_sample = p_sample.reshape(depth, n_sample, PLE_DIM)

    n_tiles = (n * TOP_K) // EXPERT_TILE + N_EXPERTS
    n_rows = n_tiles * EXPERT_TILE
    tile_start = jnp.arange(n_tiles, dtype=I32) * EXPERT_TILE
    assert n_rows % DISPATCH_TILE == 0
    expert_ids = jnp.arange(N_EXPERTS, dtype=I32)
    token_ids = jnp.arange(n * TOP_K, dtype=I32) // TOP_K
    pad_tokens = jnp.arange(n_rows, dtype=I32) % n

    x = jnp.concatenate([x_prompt.reshape(n_prompt, D_MODEL), x_sample.reshape(n_sample, D_MODEL)], axis=0)
    kp, vp, cp, hp, ks, vs, cs, hs = [], [], [], [], [], [], [], []
    for layer in range(depth):
        u, gate, q, k, v = _inproj(x, w_in_bf, layer)

        y_lru_p, h_last = _lru_prompt(u, gate, cs_zero, cs_zero, lru_p, w2, layer, batch, seq)
        y_lru_s, h_all = _lru_sample(u, gate, cs_rows[layer], h0_rows[layer], lru_p, w2,
                                     layer, n_prompt, n_sample)

        o_attn_p = _attn_prompt(q, k, v, bias_p, bias16_p, batch, seq)
        o_attn_s = _attn_sample(q, k, v, cache_kt, cache_vt, sample_tables, layer, n_prompt, dec_batch, dec_seq)

        x1, idx_t, gate_t, rank_t, cnt = _mix_route(x, y_lru_p, y_lru_s, o_attn_p, o_attn_s, w_out_bf, vec1,
                                                    wr_t, b_router3, tri, layer)

        counts = cnt[:, 0].astype(I32)
        padded = (counts + EXPERT_TILE - 1) // EXPERT_TILE * EXPERT_TILE
        ends = jnp.cumsum(padded)
        starts = ends - padded
        start_of = jnp.sum(jnp.where(idx_t[:TOP_K, :, None] == expert_ids, starts, 0), axis=-1)
        dest_flat = (start_of + rank_t[:TOP_K]).T.reshape(-1)
        tile_expert = jnp.minimum(jnp.sum((tile_start[:, None] >= ends[None, :]).astype(I32), axis=1),
                                  N_EXPERTS - 1)
        tile_valid = (tile_start < ends[-1]).astype(I32)
        tile_expert = jnp.where(tile_valid == 1, tile_expert,
                                jnp.max(jnp.where(tile_valid == 1, tile_expert, 0)))
        changed = jnp.concatenate([jnp.ones((1,), I32), (tile_expert[1:] != tile_expert[:-1]).astype(I32)])
        tile_first = changed * tile_valid

        row_token = pad_tokens.at[dest_flat].set(token_ids)
        xs = _dispatch(row_token, x1)
        ys = _experts(tile_expert, tile_first, tile_valid, xs, w_up, b_up4, w_down, b_down4, layer)
        x = _combine(dest_flat, ys, x1, gate_t[:TOP_K].T, ple_prompt, ple_sample, vec1, wg_bf, wp_bf, layer)

        k_p = k[:n_prompt].reshape(batch, seq, N_HEADS, HEAD_DIM)
        v_p = v[:n_prompt].reshape(batch, seq, N_HEADS, HEAD_DIM)
        w_prompt = min(DILATIONS[2] * ATTN_BLK, seq)
        kp.append(k_p[:, -w_prompt:])
        vp.append(v_p[:, -w_prompt:])
        cp.append(u[:n_prompt].reshape(batch, seq, LRU_WIDTH)[:, seq - (CONV_WIDTH - 1):])
        hp.append(h_last[:, 0])
        ks.append(k[n_prompt:].reshape(dec_batch, dec_seq, N_HEADS, HEAD_DIM))
        vs.append(v[n_prompt:].reshape(dec_batch, dec_seq, N_HEADS, HEAD_DIM))
        cs.append(u[n_prompt:].reshape(dec_batch, dec_seq, LRU_WIDTH)[:, dec_seq - (CONV_WIDTH - 1):])
        hs.append(h_all.reshape(dec_batch, dec_seq, LRU_WIDTH)[:, dec_seq - 1])

    y_prompt = x[:n_prompt].reshape(batch, seq, D_MODEL)
    y_sample = x[n_prompt:].reshape(dec_batch, dec_seq, D_MODEL)
    return (y_prompt, y_sample, jnp.stack(kp), jnp.stack(vp), jnp.stack(cp), jnp.stack(hp),
            jnp.stack(ks), jnp.stack(vs), jnp.stack(cs), jnp.stack(hs))
```

```python
import functools

import numpy as np
import jax
import jax.numpy as jnp
from jax import lax
from jax.experimental import pallas as pl
from jax.experimental.pallas import tpu as pltpu

F32 = jnp.float32
BF16 = jnp.bfloat16
I32 = jnp.int32

D_MODEL = 1024
LRU_WIDTH = 512
LRU_BLOCK_DIM = 64
CONV_WIDTH = 4
LRU_C = 8.0
N_HEADS = 8
HEAD_DIM = 64
ATTN_WIDTH = N_HEADS * HEAD_DIM
ATTN_SCALE = HEAD_DIM ** -0.5
ATTN_BLK = 128
DILATIONS = (1, 4, 16)
N_EXPERTS = 32
TOP_K = 4
D_FF = 1024
SWIGLU_ALPHA = 1.702
SWIGLU_LIMIT = 7.0
PLE_DIM = 256
MODEL_DEPTH = 4
DEEPNORM_ALPHA = (2.0 * MODEL_DEPTH) ** 0.25
LN_EPS = 1e-5
RMS_EPS = 1e-6

SUBLANES = 8
LANES = 128
VMEM_LIMIT_BYTES = 56 * 1024 * 1024

TOKEN_TILE = 512
LRU_CHUNK = 256
EXPERT_TILE = 512
COMBINE_TILE = 512
DISPATCH_TILE = 1024
ATTN_UNROLL = 8
DMA_UNROLL = 8


def _params(*sem):
    return pltpu.CompilerParams(dimension_semantics=sem, vmem_limit_bytes=VMEM_LIMIT_BYTES)


def _alibi_slopes():
    return [2.0 ** (-8.0 * (h + 1) / N_HEADS) for h in range(N_HEADS)]


def _inproj_kernel(x_ref, w_ref, u_ref, g_ref, q_ref, k_ref, v_ref):
    xb = x_ref[...].astype(BF16)
    for i, o_ref in enumerate((u_ref, g_ref, q_ref, k_ref, v_ref)):
        o_ref[...] = jnp.dot(xb, w_ref[0, :, i * 512:(i + 1) * 512], preferred_element_type=F32)


def _inproj(x, w_in_bf, layer):
    n = x.shape[0]
    out = jax.ShapeDtypeStruct((n, 512), F32)
    return pl.pallas_call(
        _inproj_kernel,
        grid=(n // TOKEN_TILE,),
        in_specs=[pl.BlockSpec((TOKEN_TILE, D_MODEL), lambda i: (i, 0)),
                  pl.BlockSpec((1, D_MODEL, 2560), lambda i: (layer, 0, 0))],
        out_specs=[pl.BlockSpec((TOKEN_TILE, 512), lambda i: (i, 0))] * 5,
        out_shape=[out] * 5,
        compiler_params=_params("parallel"),
        name="inproj",
    )(x, w_in_bf)


def _lru_coeffs(xc, p, w2_ref):
    xb = xc.astype(BF16)
    g0 = jnp.dot(xb[:, :256], w2_ref[0, 0], preferred_element_type=F32)
    g1 = jnp.dot(xb[:, 256:], w2_ref[0, 1], preferred_element_type=F32)
    ga = jnp.concatenate([g0[:, :256], g1[:, :256]], axis=1) + p[5:6, :]
    gx = jnp.concatenate([g0[:, 256:], g1[:, 256:]], axis=1) + p[6:7, :]
    r = jax.nn.sigmoid(ga)
    i = jax.nn.sigmoid(gx)
    z = -p[7:8, :]
    softplus = jnp.maximum(z, 0.0) + jnp.log1p(jnp.exp(-jnp.abs(z)))
    log_a = (-LRU_C) * r * softplus
    a = jnp.exp(log_a)
    b = jnp.sqrt(-jnp.tanh(log_a) * (a * a + 1.0)) * i * xc
    return a, b


def _group_scan(a, b):
    row = lax.broadcasted_iota(I32, a.shape, 0) & (SUBLANES - 1)
    for s in (1, 2, 4):
        a_prev = pltpu.roll(a, s, 0)
        b_prev = pltpu.roll(b, s, 0)
        m = row >= s
        b = jnp.where(m, a * b_prev + b, b)
        a = jnp.where(m, a * a_prev, a)
    return a, b


def _lru_finish(h, gate, p):
    y = h * jax.nn.gelu(gate)
    var = jnp.mean(y * y, axis=-1, keepdims=True)
    return (y * lax.rsqrt(var + RMS_EPS) * p[8:9, :]).astype(BF16)


def _lru_prompt_kernel(u_ref, gate_ref, cs_ref, h0_ref, p_ref, w2_ref, y_ref, hlast_ref, tail_s, hc_s):
    c = pl.program_id(1)

    @pl.when(c == 0)
    def _():
        tail_s[...] = cs_ref[0]
        hc_s[...] = h0_ref[0]

    p = p_ref[0]
    u = u_ref[...]
    tail = tail_s[...]
    row8 = lax.broadcasted_iota(I32, (SUBLANES, LRU_WIDTH), 0)
    xc = p[4:5, :] + p[3:4, :] * u
    for s in (1, 2, 3):
        sh = pltpu.roll(u, s, 0)
        first = jnp.where(row8 < s, pltpu.roll(tail, s, 0), sh[0:SUBLANES])
        xc = xc + p[3 - s:4 - s, :] * jnp.concatenate([first, sh[SUBLANES:]], axis=0)
    tail_s[...] = u[LRU_CHUNK - SUBLANES:]

    a, b = _lru_coeffs(xc, p, w2_ref)
    a, b = _group_scan(a, b)
    h = hc_s[0:1, :]
    hs = []
    for g in range(LRU_CHUNK // SUBLANES):
        sl = slice(g * SUBLANES, (g + 1) * SUBLANES)
        hg = a[sl] * h + b[sl]
        hs.append(hg)
        h = hg[SUBLANES - 1:SUBLANES, :]
    hc_s[...] = jnp.broadcast_to(h, (SUBLANES, LRU_WIDTH))
    hlast_ref[0] = jnp.broadcast_to(h, (SUBLANES, LRU_WIDTH))
    y_ref[...] = _lru_finish(jnp.concatenate(hs, axis=0), gate_ref[...], p)


def _lru_prompt(u, gate, cs8, h08, lru_p, w2, layer, batch, seq):
    n_chunks = seq // LRU_CHUNK
    n_total = batch * seq
    return pl.pallas_call(
        _lru_prompt_kernel,
        grid=(batch, n_chunks),
        in_specs=[pl.BlockSpec((LRU_CHUNK, LRU_WIDTH), lambda b, c: (b * n_chunks + c, 0)),
                  pl.BlockSpec((LRU_CHUNK, LRU_WIDTH), lambda b, c: (b * n_chunks + c, 0)),
                  pl.BlockSpec((1, SUBLANES, LRU_WIDTH), lambda b, c: (b, 0, 0)),
                  pl.BlockSpec((1, SUBLANES, LRU_WIDTH), lambda b, c: (b, 0, 0)),
                  pl.BlockSpec((1, 16, LRU_WIDTH), lambda b, c: (layer, 0, 0)),
                  pl.BlockSpec((1, 2, 256, 512), lambda b, c: (layer, 0, 0, 0))],
        out_specs=[pl.BlockSpec((LRU_CHUNK, LRU_WIDTH), lambda b, c: (b * n_chunks + c, 0)),
                   pl.BlockSpec((1, SUBLANES, LRU_WIDTH), lambda b, c: (b, 0, 0))],
        out_shape=[jax.ShapeDtypeStruct((n_total, LRU_WIDTH), BF16),
                   jax.ShapeDtypeStruct((batch, SUBLANES, LRU_WIDTH), F32)],
        scratch_shapes=[pltpu.VMEM((SUBLANES, LRU_WIDTH), F32), pltpu.VMEM((SUBLANES, LRU_WIDTH), F32)],
        compiler_params=_params("arbitrary", "arbitrary"),
        name="lru_prompt",
    )(u, gate, cs8, h08, lru_p, w2)


def _lru_sample_kernel(u_ref, gate_ref, cs_ref, h0_ref, p_ref, w2_ref, y_ref, h_ref):
    p = p_ref[0]
    u = u_ref[...]
    cs = cs_ref[...]
    rows = u.shape[0]
    row = lax.broadcasted_iota(I32, u.shape, 0) & (SUBLANES - 1)
    xc = p[4:5, :] + p[3:4, :] * u
    for s in (1, 2, 3):
        sh = pltpu.roll(u, s, 0)
        cs_sh = pltpu.roll(cs, rows - SUBLANES + s, 0)
        xc = xc + p[3 - s:4 - s, :] * jnp.where(row < s, cs_sh, sh)
    a, b = _lru_coeffs(xc, p, w2_ref)
    a, b = _group_scan(a, b)
    h = a * h0_ref[...] + b
    h_ref[...] = h
    y_ref[...] = _lru_finish(h, gate_ref[...], p)


def _lru_sample(u, gate, cs_rows, h0_rows, lru_p, w2, layer, n_prompt, n_sample):
    tile = min(256, n_sample)
    off = n_prompt // tile
    return pl.pallas_call(
        _lru_sample_kernel,
        grid=(n_sample // tile,),
        in_specs=[pl.BlockSpec((tile, LRU_WIDTH), lambda i: (off + i, 0)),
                  pl.BlockSpec((tile, LRU_WIDTH), lambda i: (off + i, 0)),
                  pl.BlockSpec((tile, LRU_WIDTH), lambda i: (i, 0)),
                  pl.BlockSpec((tile, LRU_WIDTH), lambda i: (i, 0)),
                  pl.BlockSpec((1, 16, LRU_WIDTH), lambda i: (layer, 0, 0)),
                  pl.BlockSpec((1, 2, 256, 512), lambda i: (layer, 0, 0, 0))],
        out_specs=[pl.BlockSpec((tile, LRU_WIDTH), lambda i: (i, 0)),
                   pl.BlockSpec((tile, LRU_WIDTH), lambda i: (i, 0))],
        out_shape=[jax.ShapeDtypeStruct((n_sample, LRU_WIDTH), BF16),
                   jax.ShapeDtypeStruct((n_sample, LRU_WIDTH), F32)],
        compiler_params=_params("parallel"),
        name="lru_sample",
    )(u, gate, cs_rows, h0_rows, lru_p, w2)


def _prompt_bias_tables():
    slopes = _alibi_slopes()
    q = np.arange(ATTN_BLK)[:, None]
    k = np.arange(2 * ATTN_BLK)[None, :]
    steps = q + ATTN_BLK - k
    valid = (steps >= 0) & (steps <= ATTN_BLK)
    steps_f = q - k
    valid_f = (k < ATTN_BLK) & (steps_f >= 0)
    table = np.full((N_HEADS, 2, 2, ATTN_BLK, 2 * ATTN_BLK), -np.inf, np.float32)
    table16 = np.full((N_HEADS, ATTN_BLK, ATTN_BLK), -np.inf, np.float32)
    for h in range(N_HEADS):
        for di, d in enumerate(DILATIONS[:2]):
            table[h, di, 0] = np.where(valid, -slopes[h] * d * steps, -np.inf)
            table[h, di, 1] = np.where(valid_f, -slopes[h] * d * steps_f, -np.inf)
        s16 = q - k[:, :ATTN_BLK]
        table16[h] = np.where(s16 >= 0, -slopes[h] * DILATIONS[2] * s16, -np.inf)
    return table, table16


def _attn_prompt_kernel(q_ref, k_ref, v_ref, bias_ref, bias16_ref, o_ref, acc_s, m_s, l_s, *, seq):
    lane = lax.broadcasted_iota(I32, (ATTN_BLK, LANES), 1)
    lo = lane < HEAD_DIM

    def rows(start, size, stride):
        return pl.ds(start, size) if stride == 1 else pl.ds(start, size, stride=stride)

    def tile(di, qstart, kstart, nk, stride, bias_of_head):
        qsl = rows(qstart, ATTN_BLK, stride)
        ksl = rows(kstart, nk, stride)
        qs = q_ref[qsl, :] * ATTN_SCALE
        kk = k_ref[ksl, :].astype(BF16)
        vv = v_ref[ksl, :].astype(BF16)
        parts = []
        for hh in (0, 1):
            qm = jnp.where(lo if hh == 0 else jnp.logical_not(lo), qs, 0.0).astype(BF16)
            s = lax.dot_general(qm, kk, (((1,), (1,)), ((), ())), preferred_element_type=F32)
            s = s + bias_of_head(hh)
            m = jnp.max(s, axis=-1, keepdims=True)
            e = jnp.exp(s - m)
            l = jnp.sum(e, axis=-1, keepdims=True)
            pv = jnp.dot(e.astype(BF16), vv, preferred_element_type=F32)
            parts.append((pv, m, l))
        (pv0, m0, l0), (pv1, m1, l1) = parts
        acc_s[di, qsl, :] = jnp.where(lo, pv0, pv1)
        m_s[di, qsl, :] = jnp.where(lo, m0, m1)
        l_s[di, qsl, :] = jnp.where(lo, l0, l1)

    d1, d4, d16 = DILATIONS
    span4 = d4 * ATTN_BLK

    def body1(it, carry):
        for u in range(ATTN_UNROLL):
            j = it * ATTN_UNROLL + u
            qstart = pl.multiple_of(ATTN_BLK * j, ATTN_BLK)
            if u == 0:
                first = 1 - jnp.minimum(j, 1)
                kstart = pl.multiple_of(ATTN_BLK * jnp.maximum(j - 1, 0), ATTN_BLK)
            else:
                first = 0
                kstart = pl.multiple_of(ATTN_BLK * (j - 1), ATTN_BLK)
            tile(0, qstart, kstart, 2 * ATTN_BLK, d1, lambda hh, first=first: bias_ref[hh, 0, first])
        return carry

    lax.fori_loop(0, seq // (ATTN_BLK * ATTN_UNROLL), body1, 0)

    def body4(r, carry):
        for j in range(seq // span4):
            tile(1, r + span4 * j, r + span4 * max(j - 1, 0), 2 * ATTN_BLK, d4,
                 lambda hh, first=int(j == 0): bias_ref[hh, 1, first])
        return carry

    lax.fori_loop(0, d4, body4, 0)

    def body16(it, carry):
        for u in range(ATTN_UNROLL):
            r = it * ATTN_UNROLL + u
            tile(2, r, r, ATTN_BLK, d16, lambda hh: bias16_ref[hh])
        return carry

    lax.fori_loop(0, d16 // ATTN_UNROLL, body16, 0)

    chunk = 256

    def merge(ci, carry):
        sl = pl.ds(pl.multiple_of(ci * chunk, chunk), chunk)
        m0, m1, m2 = m_s[0, sl, :], m_s[1, sl, :], m_s[2, sl, :]
        mm = jnp.maximum(jnp.maximum(m0, m1), m2)
        w0, w1, w2 = jnp.exp(m0 - mm), jnp.exp(m1 - mm), jnp.exp(m2 - mm)
        den = w0 * l_s[0, sl, :] + w1 * l_s[1, sl, :] + w2 * l_s[2, sl, :]
        num = w0 * acc_s[0, sl, :] + w1 * acc_s[1, sl, :] + w2 * acc_s[2, sl, :]
        o_ref[sl, :] = num / den
        return carry

    lax.fori_loop(0, seq // chunk, merge, 0)


def _attn_prompt(q, k, v, bias, bias16, batch, seq):
    assert seq == DILATIONS[2] * ATTN_BLK, "dilation-16 pass assumes one tile per residue class"
    hp = N_HEADS // 2
    n_total = batch * seq
    qkv_spec = pl.BlockSpec((seq, LANES), lambda b, p: (b, p))
    return pl.pallas_call(
        functools.partial(_attn_prompt_kernel, seq=seq),
        grid=(batch, hp),
        in_specs=[qkv_spec, qkv_spec, qkv_spec,
                  pl.BlockSpec((2, 2, 2, ATTN_BLK, 2 * ATTN_BLK), lambda b, p: (p, 0, 0, 0, 0)),
                  pl.BlockSpec((2, ATTN_BLK, ATTN_BLK), lambda b, p: (p, 0, 0))],
        out_specs=pl.BlockSpec((seq, LANES), lambda b, p: (b, p)),
        out_shape=jax.ShapeDtypeStruct((n_total, ATTN_WIDTH), F32),
        scratch_shapes=[pltpu.VMEM((3, seq, LANES), F32)] * 3,
        compiler_params=_params("parallel", "parallel"),
        name="attn_prompt",
    )(q, k, v, bias, bias16)


def _sample_tables(w_buf, dec_seq):
    slopes = np.asarray(_alibi_slopes())
    rows = np.arange(N_HEADS * dec_seq)
    h, t = rows // dec_seq, rows % dec_seq

    def table(dist, real):
        cnt = np.zeros(dist.shape, np.float64)
        for d in DILATIONS:
            cnt += real & (dist >= 0) & (dist % d == 0) & (dist <= d * ATTN_BLK)
        return np.where(cnt > 0, -slopes[h][:, None] * dist + np.log(np.maximum(cnt, 1.0)),
                        -np.inf).astype(np.float32)

    keys = np.arange(w_buf)[None, :]
    own = np.arange(LANES)[None, :]
    return (table(w_buf + t[:, None] - keys, keys >= 0),
            table(t[:, None] - own, own < dec_seq))


def _attn_sample_kernel(q_ref, k_ref, v_ref, ckt_ref, cvt_ref, bias_ref, biasn_ref, o_ref, *, dec_seq):
    n_rows = N_HEADS * dec_seq
    w_buf = ckt_ref.shape[-1]
    q = q_ref[...] * ATTN_SCALE
    qt = jnp.concatenate([q] * N_HEADS, axis=0)
    own = (lax.broadcasted_iota(I32, (n_rows, ATTN_WIDTH), 0) // dec_seq
           == lax.broadcasted_iota(I32, (n_rows, ATTN_WIDTH), 1) // HEAD_DIM)
    qm = jnp.where(own, qt, 0.0).astype(BF16)
    nt = (((1,), (1,)), ((), ()))
    kct = ckt_ref[0, 0].reshape(ATTN_WIDTH, w_buf).astype(BF16)
    s_c = jnp.dot(qm, kct, preferred_element_type=F32) + bias_ref[...]
    pad = jnp.zeros((LANES - dec_seq, ATTN_WIDTH), F32)
    kn = jnp.concatenate([k_ref[...], pad], axis=0).astype(BF16)
    vn = jnp.concatenate([v_ref[...], pad], axis=0).astype(BF16)
    s_n = lax.dot_general(qm, kn, nt, preferred_element_type=F32) + biasn_ref[...]
    m = jnp.maximum(jnp.max(s_c, axis=-1, keepdims=True), jnp.max(s_n, axis=-1, keepdims=True))
    e_c = jnp.exp(s_c - m)
    e_n = jnp.exp(s_n - m)
    l = jnp.sum(e_c, axis=-1, keepdims=True) + jnp.sum(e_n, axis=-1, keepdims=True)
    vct = cvt_ref[0, 0].reshape(ATTN_WIDTH, w_buf).astype(BF16)
    r = (lax.dot_general(e_c.astype(BF16), vct, nt, preferred_element_type=F32)
         + jnp.dot(e_n.astype(BF16), vn, preferred_element_type=F32))
    r = jnp.where(own, r, 0.0) / l
    o = r[0:dec_seq]
    for h in range(1, N_HEADS):
        o = o + r[h * dec_seq:(h + 1) * dec_seq]
    o_ref[...] = o


def _attn_sample(q, k, v, cache_kt, cache_vt, tables, layer, n_prompt, dec_batch, dec_seq):
    w_buf = cache_kt.shape[-1]
    off = n_prompt // dec_seq
    n_rows = N_HEADS * dec_seq
    row_spec = pl.BlockSpec((dec_seq, ATTN_WIDTH), lambda b: (off + b, 0))
    cache_spec = pl.BlockSpec((1, 1, N_HEADS, HEAD_DIM, w_buf), lambda b: (layer, b, 0, 0, 0))
    bias, bias_n = tables
    return pl.pallas_call(
        functools.partial(_attn_sample_kernel, dec_seq=dec_seq),
        grid=(dec_batch,),
        in_specs=[row_spec, row_spec, row_spec, cache_spec, cache_spec,
                  pl.BlockSpec((n_rows, w_buf), lambda b: (0, 0)),
                  pl.BlockSpec((n_rows, LANES), lambda b: (0, 0))],
        out_specs=pl.BlockSpec((dec_seq, ATTN_WIDTH), lambda b: (b, 0)),
        out_shape=jax.ShapeDtypeStruct((dec_batch * dec_seq, ATTN_WIDTH), F32),
        compiler_params=_params("parallel"),
        name="attn_sample",
    )(q, k, v, cache_kt, cache_vt, bias, bias_n)


def _layer_norm(z, g, b):
    mu = jnp.mean(z, axis=-1, keepdims=True)
    zc = z - mu
    var = jnp.mean(zc * zc, axis=-1, keepdims=True)
    return zc * lax.rsqrt(var + LN_EPS) * g + b


def _mix_route_kernel(x_ref, ylp_ref, yls_ref, oap_ref, oas_ref, wout_ref, vec_ref, wr_ref, br_ref, tri_ref,
                      x1_ref, idx_ref, gate_ref, rank_ref, cnt_ref, carry_s, *, n_prompt_tiles):
    i = pl.program_id(0)

    @pl.when(i == 0)
    def _():
        carry_s[...] = jnp.zeros_like(carry_s)

    vec = vec_ref[0]
    is_prompt = i < n_prompt_tiles
    oa = jnp.where(is_prompt, oap_ref[...], oas_ref[...])
    yl = jnp.where(is_prompt, ylp_ref[...], yls_ref[...])
    oa_n = (oa * lax.rsqrt(jnp.mean(oa * oa, axis=-1, keepdims=True) + RMS_EPS) * vec[0:1, :ATTN_WIDTH])
    mixed = (jnp.dot(yl, wout_ref[0, :LRU_WIDTH], preferred_element_type=F32)
             + jnp.dot(oa_n.astype(BF16), wout_ref[0, LRU_WIDTH:], preferred_element_type=F32))
    x1 = _layer_norm(DEEPNORM_ALPHA * x_ref[...] + mixed, vec[1:2, :], vec[2:3, :])
    x1_ref[...] = x1

    logits = lax.dot_general(wr_ref[0], x1.astype(BF16), (((1,), (1,)), ((), ())),
                             preferred_element_type=F32) + br_ref[0]
    tile = logits.shape[1]
    e_iota = lax.broadcasted_iota(I32, logits.shape, 0).astype(F32)
    work = logits
    vals, idxs, sels = [], [], []
    for _ in range(TOP_K):
        mx = jnp.max(work, axis=0, keepdims=True)
        idx = jnp.min(jnp.where(work == mx, e_iota, float(N_EXPERTS)), axis=0, keepdims=True)
        sel = e_iota == idx
        vals.append(mx)
        idxs.append(idx)
        sels.append(sel)
        work = jnp.where(sel, -jnp.inf, work)
    exps = [jnp.exp(v - vals[0]) for v in vals]
    den = exps[0] + exps[1] + exps[2] + exps[3]
    gates = [e / den for e in exps]

    onehot = jnp.zeros(logits.shape, F32)
    for sel in sels:
        onehot = onehot + sel.astype(F32)
    prefix = jnp.dot(onehot.astype(BF16), tri_ref[...], preferred_element_type=F32) + carry_s[:, 0:1]
    ranks = [jnp.sum(jnp.where(sel, prefix, 0.0), axis=0, keepdims=True) for sel in sels]
    carry = carry_s[:, 0:1] + jnp.sum(onehot, axis=1, keepdims=True)
    carry_s[...] = jnp.broadcast_to(carry, carry_s.shape)
    cnt_ref[...] = jnp.broadcast_to(carry, cnt_ref.shape)

    row = lax.broadcasted_iota(I32, (SUBLANES, tile), 0)

    def stack(parts):
        out = jnp.zeros((SUBLANES, tile), F32)
        for k_, part in enumerate(parts):
            out = jnp.where(row == k_, part, out)
        return out

    idx_ref[...] = stack(idxs).astype(I32)
    gate_ref[...] = stack(gates)
    rank_ref[...] = stack(ranks).astype(I32)


def _mix_route(x, y_lru_p, y_lru_s, o_attn_p, o_attn_s, w_out_bf, vec1, wr_t, b_router, tri, layer):
    n = x.shape[0]
    t = TOKEN_TILE
    n_prompt_tiles = y_lru_p.shape[0] // t
    assert y_lru_p.shape[0] % t == 0 and y_lru_s.shape[0] % t == 0
    row = lambda w: pl.BlockSpec((t, w), lambda i: (i, 0))
    prompt_row = lambda w: pl.BlockSpec((t, w), lambda i: (jnp.minimum(i, n_prompt_tiles - 1), 0))
    sample_row = lambda w: pl.BlockSpec((t, w), lambda i: (jnp.maximum(i - n_prompt_tiles, 0), 0))
    col = pl.BlockSpec((SUBLANES, t), lambda i: (0, i))
    return pl.pallas_call(
        functools.partial(_mix_route_kernel, n_prompt_tiles=n_prompt_tiles),
        grid=(n // t,),
        in_specs=[row(D_MODEL), prompt_row(LRU_WIDTH), sample_row(LRU_WIDTH),
                  prompt_row(ATTN_WIDTH), sample_row(ATTN_WIDTH),
                  pl.BlockSpec((1, D_MODEL, D_MODEL), lambda i: (layer, 0, 0)),
                  pl.BlockSpec((1, SUBLANES, D_MODEL), lambda i: (layer, 0, 0)),
                  pl.BlockSpec((1, N_EXPERTS, D_MODEL), lambda i: (layer, 0, 0)),
                  pl.BlockSpec((1, N_EXPERTS, 1), lambda i: (layer, 0, 0)),
                  pl.BlockSpec((t, t), lambda i: (0, 0))],
        out_specs=[row(D_MODEL), col, col, col,
                   pl.BlockSpec((N_EXPERTS, LANES), lambda i: (0, 0))],
        out_shape=[jax.ShapeDtypeStruct((n, D_MODEL), F32),
                   jax.ShapeDtypeStruct((SUBLANES, n), I32),
                   jax.ShapeDtypeStruct((SUBLANES, n), F32),
                   jax.ShapeDtypeStruct((SUBLANES, n), I32),
                   jax.ShapeDtypeStruct((N_EXPERTS, LANES), F32)],
        scratch_shapes=[pltpu.VMEM((N_EXPERTS, LANES), F32)],
        compiler_params=_params("arbitrary"),
        name="mix_route",
    )(x, y_lru_p, y_lru_s, o_attn_p, o_attn_s, w_out_bf, vec1, wr_t, b_router, tri)


def _fetch_indices(dest_hbm, idx_s, idx_sem, n_idx):
    i = pl.program_id(0)
    slot = i % 2

    def idx_copy(step, sl):
        return pltpu.make_async_copy(dest_hbm.at[pl.ds(pl.multiple_of(step * n_idx, n_idx), n_idx)],
                                     idx_s.at[pl.ds(pl.multiple_of(sl * n_idx, n_idx), n_idx)], idx_sem.at[sl])

    @pl.when(i == 0)
    def _():
        idx_copy(0, 0).start()

    idx_copy(i, slot).wait()

    @pl.when(i + 1 < pl.num_programs(0))
    def _():
        idx_copy(i + 1, 1 - slot).start()

    return slot * n_idx


def _dispatch_kernel(tok_hbm, x_hbm, xs_ref, idx_s, idx_sem, sem):
    off = _fetch_indices(tok_hbm, idx_s, idx_sem, DISPATCH_TILE)

    def row_copy(it, j):
        token = idx_s[off + it * SUBLANES + j]
        return pltpu.make_async_copy(x_hbm.at[pl.ds(token, 1), :], xs_ref.at[it, pl.ds(j, 1), :], sem)

    def issue(it, carry):
        for j in range(SUBLANES):
            row_copy(it, j).start(priority=j % 2)
        return carry

    lax.fori_loop(0, DISPATCH_TILE // SUBLANES, issue, 0)

    def drain(it, carry):
        for j in range(SUBLANES):
            row_copy(it, j).wait()
        return carry

    lax.fori_loop(0, DISPATCH_TILE // SUBLANES, drain, 0)


def _dispatch(row_token, x1):
    n_rows = row_token.shape[0]
    xs = pl.pallas_call(
        _dispatch_kernel,
        grid=(n_rows // DISPATCH_TILE,),
        in_specs=[pl.BlockSpec(memory_space=pl.ANY), pl.BlockSpec(memory_space=pl.ANY)],
        out_specs=pl.BlockSpec((DISPATCH_TILE // SUBLANES, SUBLANES, D_MODEL), lambda i: (i, 0, 0)),
        out_shape=jax.ShapeDtypeStruct((n_rows // SUBLANES, SUBLANES, D_MODEL), F32),
        scratch_shapes=[pltpu.SMEM((2 * DISPATCH_TILE,), I32),
                        pltpu.SemaphoreType.DMA((2,)), pltpu.SemaphoreType.DMA(())],
        compiler_params=_params("arbitrary"),
        name="dispatch",
    )(row_token, x1)
    return xs.reshape(n_rows, D_MODEL)


def _expert_kernel(te_ref, tf_ref, tv_ref, xs_ref, wu_ref, bu_ref, wd_ref, bd_ref, ys_ref, wub_s, wdb_s):
    del te_ref
    i = pl.program_id(0)

    @pl.when(tf_ref[i] == 1)
    def _():
        wub_s[...] = wu_ref[0, 0].astype(BF16)
        wdb_s[...] = wd_ref[0, 0].astype(BF16)

    @pl.when(tv_ref[i] == 0)
    def _():
        ys_ref[...] = jnp.zeros_like(ys_ref)

    @pl.when(tv_ref[i] == 1)
    def _():
        xb = xs_ref[...].astype(BF16)
        h = jnp.dot(xb, wub_s[...], preferred_element_type=F32) + bu_ref[0, 0]
        glu = jnp.minimum(h[:, :D_FF], SWIGLU_LIMIT)
        lin = jnp.clip(h[:, D_FF:], -SWIGLU_LIMIT, SWIGLU_LIMIT)
        act = glu * jax.nn.sigmoid(SWIGLU_ALPHA * glu) * (lin + 1.0)
        ys_ref[...] = jnp.dot(act.astype(BF16), wdb_s[...], preferred_element_type=F32) + bd_ref[0, 0]


def _experts(tile_expert, tile_first, tile_valid, xs, w_up, b_up4, w_down, b_down4, layer):
    n_rows = xs.shape[0]
    n_tiles = n_rows // EXPERT_TILE
    grid_spec = pltpu.PrefetchScalarGridSpec(
        num_scalar_prefetch=3,
        grid=(n_tiles,),
        in_specs=[pl.BlockSpec((EXPERT_TILE, D_MODEL), lambda i, te, tf, tv: (i, 0)),
                  pl.BlockSpec((1, 1, D_MODEL, 2 * D_FF), lambda i, te, tf, tv: (layer, te[i], 0, 0)),
                  pl.BlockSpec((1, 1, 1, 2 * D_FF), lambda i, te, tf, tv: (layer, te[i], 0, 0)),
                  pl.BlockSpec((1, 1, D_FF, D_MODEL), lambda i, te, tf, tv: (layer, te[i], 0, 0)),
                  pl.BlockSpec((1, 1, 1, D_MODEL), lambda i, te, tf, tv: (layer, te[i], 0, 0))],
        out_specs=pl.BlockSpec((EXPERT_TILE, D_MODEL), lambda i, te, tf, tv: (i, 0)),
        scratch_shapes=[pltpu.VMEM((D_MODEL, 2 * D_FF), BF16), pltpu.VMEM((D_FF, D_MODEL), BF16)],
    )
    return pl.pallas_call(
        _expert_kernel,
        grid_spec=grid_spec,
        out_shape=jax.ShapeDtypeStruct((n_rows, D_MODEL), F32),
        compiler_params=_params("arbitrary"),
        name="experts",
    )(tile_expert, tile_first, tile_valid, xs, w_up, b_up4, w_down, b_down4)


def _combine_kernel(dest_hbm, ys_hbm, x1_ref, gate_ref, pp_ref, ps_ref, vec_ref, wg_ref, wp_ref, o_ref,
                    idx_s, buf_s, idx_sem, sem, *, n_prompt_tiles):
    off = _fetch_indices(dest_hbm, idx_s, idx_sem, TOP_K * COMBINE_TILE)

    def row_copy(it, j, k):
        row = idx_s[off + (it * SUBLANES + j) * TOP_K + k]
        return pltpu.make_async_copy(ys_hbm.at[pl.ds(row, 1), :], buf_s.at[k, it, pl.ds(j, 1), :], sem)

    def issue(it, carry):
        for j in range(SUBLANES):
            for k in range(TOP_K):
                row_copy(it, j, k).start(priority=k % 2)
        return carry

    lax.fori_loop(0, COMBINE_TILE // SUBLANES, issue, 0)

    def drain(it, carry):
        for j in range(SUBLANES):
            for k in range(TOP_K):
                row_copy(it, j, k).wait()
        return carry

    lax.fori_loop(0, COMBINE_TILE // SUBLANES, drain, 0)

    vec = vec_ref[0]
    g = gate_ref[...]
    rows = lambda k: buf_s[k].reshape(COMBINE_TILE, D_MODEL)
    ffn = g[:, 0:1] * rows(0)
    for k in range(1, TOP_K):
        ffn = ffn + g[:, k:k + 1] * rows(k)
    x2 = _layer_norm(DEEPNORM_ALPHA * x1_ref[...] + ffn, vec[3:4, :], vec[4:5, :])
    pg = jax.nn.sigmoid(jnp.dot(x2.astype(BF16), wg_ref[0], preferred_element_type=F32))
    ple = jnp.where(pl.program_id(0) < n_prompt_tiles, pp_ref[0], ps_ref[0])
    pp = jnp.dot(ple.astype(BF16), wp_ref[0], preferred_element_type=F32)
    o_ref[...] = x2 + pg * pp


def _combine(dest_flat, ys, x1, gates_col, ple_prompt, ple_sample, vec1, wg_bf, wp_bf, layer):
    n = x1.shape[0]
    t = COMBINE_TILE
    n_prompt_tiles = ple_prompt.shape[1] // t
    assert ple_prompt.shape[1] % t == 0 and ple_sample.shape[1] % t == 0
    return pl.pallas_call(
        functools.partial(_combine_kernel, n_prompt_tiles=n_prompt_tiles),
        grid=(n // t,),
        in_specs=[pl.BlockSpec(memory_space=pl.ANY),
                  pl.BlockSpec(memory_space=pl.ANY),
                  pl.BlockSpec((t, D_MODEL), lambda i: (i, 0)),
                  pl.BlockSpec((t, TOP_K), lambda i: (i, 0)),
                  pl.BlockSpec((1, t, PLE_DIM), lambda i: (layer, jnp.minimum(i, n_prompt_tiles - 1), 0)),
                  pl.BlockSpec((1, t, PLE_DIM), lambda i: (layer, jnp.maximum(i - n_prompt_tiles, 0), 0)),
                  pl.BlockSpec((1, SUBLANES, D_MODEL), lambda i: (layer, 0, 0)),
                  pl.BlockSpec((1, D_MODEL, D_MODEL), lambda i: (layer, 0, 0)),
                  pl.BlockSpec((1, PLE_DIM, D_MODEL), lambda i: (layer, 0, 0))],
        out_specs=pl.BlockSpec((t, D_MODEL), lambda i: (i, 0)),
        out_shape=jax.ShapeDtypeStruct((n, D_MODEL), F32),
        scratch_shapes=[pltpu.SMEM((2 * TOP_K * t,), I32),
                        pltpu.VMEM((TOP_K, t // SUBLANES, SUBLANES, D_MODEL), F32),
                        pltpu.SemaphoreType.DMA((2,)), pltpu.SemaphoreType.DMA(())],
        compiler_params=_params("arbitrary"),
        name="combine",
    )(dest_flat, ys, x1, gates_col, ple_prompt, ple_sample, vec1, wg_bf, wp_bf)


def _block_diag_pairs(w_a, w_x):
    depth = w_a.shape[0]
    eye = jnp.eye(4, dtype=w_a.dtype)

    def halves(w):
        w = w.reshape(depth, 2, 4, LRU_BLOCK_DIM, LRU_BLOCK_DIM)
        return jnp.einsum('lhnij,nm->lhnimj', w, eye).reshape(depth, 2, 256, 256)

    return jnp.concatenate([halves(w_a), halves(w_x)], axis=-1).astype(BF16)


def kernel(x_prompt, x_sample, cache_k, cache_v, state_conv, state_h, p_prompt, p_sample, w_in, conv_w, conv_b,
           w_gate_a, b_gate_a, w_gate_x, b_gate_x, lru_lambda, g_lru_norm, g_attn_norm, w_out, ln1_g, ln1_b,
           w_router, b_router, w_up, b_up, w_down, b_down, ln2_g, ln2_b, w_ple_proj, w_ple_gate):
    depth = w_in.shape[0]
    batch, seq, _ = x_prompt.shape
    dec_batch, dec_seq, _ = x_sample.shape
    w_buf = cache_k.shape[2]
    n_prompt, n_sample = batch * seq, dec_batch * dec_seq
    n = n_prompt + n_sample
    assert dec_seq == SUBLANES and seq % LRU_CHUNK == 0 and n % TOKEN_TILE == 0 and n_prompt % TOKEN_TILE == 0

    w_in_bf = w_in.astype(BF16)
    w_out_bf = w_out.astype(BF16)
    wg_bf = w_ple_gate.astype(BF16)
    wp_bf = w_ple_proj.astype(BF16)
    wr_t = jnp.swapaxes(w_router, 1, 2).astype(BF16)
    b_router3 = b_router[:, :, None]
    w2 = _block_diag_pairs(w_gate_a, w_gate_x)
    zeros512 = jnp.zeros((depth, 7, LRU_WIDTH), F32)
    lru_p = jnp.concatenate([conv_w, conv_b[:, None], b_gate_a[:, None], b_gate_x[:, None], lru_lambda[:, None],
                             g_lru_norm[:, None], zeros512], axis=1)
    pad_attn = jnp.zeros((depth, D_MODEL - ATTN_WIDTH), F32)
    vec1 = jnp.stack([jnp.concatenate([g_attn_norm, pad_attn], axis=1), ln1_g, ln1_b, ln2_g, ln2_b,
                      jnp.zeros_like(ln1_g), jnp.zeros_like(ln1_g), jnp.zeros_like(ln1_g)], axis=1)
    b_up4 = b_up[:, :, None, :]
    b_down4 = b_down[:, :, None, :]
    tri = jnp.asarray(np.triu(np.ones((TOKEN_TILE, TOKEN_TILE), np.float32), k=1), BF16)
    bias_np, bias16_np = _prompt_bias_tables()
    bias_p, bias16_p = jnp.asarray(bias_np), jnp.asarray(bias16_np)
    sample_tables = tuple(jnp.asarray(t) for t in _sample_tables(w_buf, dec_seq))
    cache_kt = jnp.transpose(cache_k, (0, 1, 3, 4, 2))
    cache_vt = jnp.transpose(cache_v, (0, 1, 3, 4, 2))

    cs_rows = jnp.pad(state_conv, ((0, 0), (0, 0), (SUBLANES - (CONV_WIDTH - 1), 0), (0, 0))
                      ).reshape(depth, n_sample, LRU_WIDTH)
    h0_rows = jnp.repeat(state_h, dec_seq, axis=1)
    cs_zero = jnp.zeros((batch, SUBLANES, LRU_WIDTH), F32)
    ple_prompt = p_prompt.reshape(depth, n_prompt, PLE_DIM)
    ple_sample = p_sample.reshape(depth, n_sample, PLE_DIM)

    n_tiles = (n * TOP_K) // EXPERT_TILE + N_EXPERTS
    n_rows = n_tiles * EXPERT_TILE
    tile_start = jnp.arange(n_tiles, dtype=I32) * EXPERT_TILE
    assert n_rows % DISPATCH_TILE == 0
    expert_ids = jnp.arange(N_EXPERTS, dtype=I32)
    token_ids = jnp.arange(n * TOP_K, dtype=I32) // TOP_K
    pad_tokens = jnp.arange(n_rows, dtype=I32) % n

    x = jnp.concatenate([x_prompt.reshape(n_prompt, D_MODEL), x_sample.reshape(n_sample, D_MODEL)], axis=0)
    kp, vp, cp, hp, ks, vs, cs, hs = [], [], [], [], [], [], [], []
    for layer in range(depth):
        u, gate, q, k, v = _inproj(x, w_in_bf, layer)

        y_lru_p, h_last = _lru_prompt(u, gate, cs_zero, cs_zero, lru_p, w2, layer, batch, seq)
        y_lru_s, h_all = _lru_sample(u, gate, cs_rows[layer], h0_rows[layer], lru_p, w2,
                                     layer, n_prompt, n_sample)

        o_attn_p = _attn_prompt(q, k, v, bias_p, bias16_p, batch, seq)
        o_attn_s = _attn_sample(q, k, v, cache_kt, cache_vt, sample_tables, layer, n_prompt, dec_batch, dec_seq)

        x1, idx_t, gate_t, rank_t, cnt = _mix_route(x, y_lru_p, y_lru_s, o_attn_p, o_attn_s, w_out_bf, vec1,
                                                    wr_t, b_router3, tri, layer)

        counts = cnt[:, 0].astype(I32)
        padded = (counts + EXPERT_TILE - 1) // EXPERT_TILE * EXPERT_TILE
        ends = jnp.cumsum(padded)
        starts = ends - padded
        start_of = jnp.sum(jnp.where(idx_t[:TOP_K, :, None] == expert_ids, starts, 0), axis=-1)
        dest_flat = (start_of + rank_t[:TOP_K]).T.reshape(-1)
        tile_expert = jnp.minimum(jnp.sum((tile_start[:, None] >= ends[None, :]).astype(I32), axis=1),
                                  N_EXPERTS - 1)
        tile_valid = (tile_start < ends[-1]).astype(I32)
        tile_expert = jnp.where(tile_valid == 1, tile_expert,
                                jnp.max(jnp.where(tile_valid == 1, tile_expert, 0)))
        changed = jnp.concatenate([jnp.ones((1,), I32), (tile_expert[1:] != tile_expert[:-1]).astype(I32)])
        tile_first = changed * tile_valid

        row_token = pad_tokens.at[dest_flat].set(token_ids)
        xs = _dispatch(row_token, x1)
        ys = _experts(tile_expert, tile_first, tile_valid, xs, w_up, b_up4, w_down, b_down4, layer)
        x = _combine(dest_flat, ys, x1, gate_t[:TOP_K].T, ple_prompt, ple_sample, vec1, wg_bf, wp_bf, layer)

        k_p = k[:n_prompt].reshape(batch, seq, N_HEADS, HEAD_DIM)
        v_p = v[:n_prompt].reshape(batch, seq, N_HEADS, HEAD_DIM)
        w_prompt = min(DILATIONS[2] * ATTN_BLK, seq)
        kp.append(k_p[:, -w_prompt:])
        vp.append(v_p[:, -w_prompt:])
        cp.append(u[:n_prompt].reshape(batch, seq, LRU_WIDTH)[:, seq - (CONV_WIDTH - 1):])
        hp.append(h_last[:, 0])
        ks.append(k[n_prompt:].reshape(dec_batch, dec_seq, N_HEADS, HEAD_DIM))
        vs.append(v[n_prompt:].reshape(dec_batch, dec_seq, N_HEADS, HEAD_DIM))
        cs.append(u[n_prompt:].reshape(dec_batch, dec_seq, LRU_WIDTH)[:, dec_seq - (CONV_WIDTH - 1):])
        hs.append(h_all.reshape(dec_batch, dec_seq, LRU_WIDTH)[:, dec_seq - 1])

    y_prompt = x[:n_prompt].reshape(batch, seq, D_MODEL)
    y_sample = x[n_prompt:].reshape(dec_batch, dec_seq, D_MODEL)
    return (y_prompt, y_sample, jnp.stack(kp), jnp.stack(vp), jnp.stack(cp), jnp.stack(hp),
            jnp.stack(ks), jnp.stack(vs), jnp.stack(cs), jnp.stack(hs))
```
